```python
import jax, jax.numpy as jnp
from jax import lax
import numpy as np

D_MODEL = 1024
BATCH = 8
SEQ = 4096
DEPTH = 1

CHUNK = 64
Q_BLOCK = 2 * CHUNK
POOL_WINDOWS = (2, 4, 8, 16)
N_POOL_GROUPS = len(POOL_WINDOWS)
POOL_WIDTH = D_MODEL // 2
POOL_GROUP = POOL_WIDTH // N_POOL_GROUPS
HEAD_DIM = 64
ATT_WIDTH = D_MODEL // 2
N_HEADS = ATT_WIDTH // HEAD_DIM
N_BRANCHES = 2
IN_SPLITS = (
    POOL_WIDTH,
    POOL_WIDTH + ATT_WIDTH,
    POOL_WIDTH + 2 * ATT_WIDTH,
    POOL_WIDTH + 3 * ATT_WIDTH,
    POOL_WIDTH + 3 * ATT_WIDTH + N_HEADS,
)
IN_WIDTH = POOL_WIDTH + 3 * ATT_WIDTH + N_HEADS + N_BRANCHES * D_MODEL
D_FF = 256 * ((8 * D_MODEL // 3 + 255) // 256)
PLE_DIM = 256
ALPHA = (2 * DEPTH) ** 0.25
BETA = (8 * DEPTH) ** -0.25
LN_EPS = 1e-5

kernel_name = "hybrid_pool_fox_macaron_deepnorm"


def layer_norm(x, g, b):
    xf = x.astype(jnp.float32)
    mu = jnp.mean(xf, axis=-1, keepdims=True)
    var = jnp.mean(jnp.square(xf - mu), axis=-1, keepdims=True)
    y = (xf - mu) * lax.rsqrt(var + LN_EPS)
    return (y * g + b).astype(x.dtype)


def swiglu(h, w_gate, w_up, w_down):
    return (jax.nn.silu(h @ w_gate) * (h @ w_up)) @ w_down


def pool_mixer(u, w_group, scale):
    B, S, _ = u.shape
    uf = u.astype(jnp.float32)
    csum = jnp.cumsum(uf, axis=1)
    t = jnp.arange(S)
    res = []
    for g, w in enumerate(POOL_WINDOWS):
        sl = slice(g * POOL_GROUP, (g + 1) * POOL_GROUP)
        c_hi = csum[:, :, sl]
        c_lo = jnp.pad(c_hi[:, :S - w], ((0, 0), (w, 0), (0, 0)))
        cnt = jnp.minimum(t + 1, w).astype(jnp.float32)[None, :, None]
        res.append((c_hi - c_lo) / cnt - uf[:, :, sl])
    r = jnp.stack(res, axis=2).astype(u.dtype)
    y = jnp.einsum('bsgc,gcd->bsgd', r, w_group).reshape(B, S, POOL_WIDTH)
    return y * scale


def forgetting_attention(q, k, v, logf):
    B, S, H, Dh = q.shape
    scale = Dh ** -0.5
    F = jnp.cumsum(logf, axis=1).transpose(0, 2, 1)
    qh = q.transpose(0, 2, 1, 3)
    kh = k.transpose(0, 2, 1, 3)
    vh = v.transpose(0, 2, 1, 3)
    outs = []
    for start in range(0, S, Q_BLOCK):
        end = start + Q_BLOCK
        s = jnp.einsum('bhqd,bhkd->bhqk', qh[:, :, start:end], kh[:, :, :end]).astype(jnp.float32)
        s = s * scale + F[:, :, start:end, None] - F[:, :, None, :end]
        mask = jnp.arange(start, end)[:, None] >= jnp.arange(end)[None, :]
        s = jnp.where(mask[None, None], s, -jnp.inf)
        pr = jax.nn.softmax(s, axis=-1).astype(vh.dtype)
        outs.append(jnp.einsum('bhqk,bhkd->bhqd', pr, vh[:, :, :end]))
    o = jnp.concatenate(outs, axis=2)
    return o.transpose(0, 2, 1, 3).reshape(B, S, H * Dh)


def mixer(h, w_in, b_f, b_gate, pool_w_group, pool_scale, pool_w_up, att_w_up, w_out):
    B, S, _ = h.shape
    z = h @ w_in
    u, q, k, v, fl, gl = jnp.split(z, IN_SPLITS, axis=-1)
    q = q.reshape(B, S, N_HEADS, HEAD_DIM)
    k = k.reshape(B, S, N_HEADS, HEAD_DIM)
    v = v.reshape(B, S, N_HEADS, HEAD_DIM)
    logf = jax.nn.log_sigmoid(fl.astype(jnp.float32) + b_f)
    y_pool = pool_mixer(u, pool_w_group, pool_scale) @ pool_w_up
    y_att = forgetting_attention(q, k, v, logf) @ att_w_up
    g_pool, g_att = jnp.split(jax.nn.sigmoid(gl + b_gate), N_BRANCHES, axis=-1)
    return (g_pool * y_pool + g_att * y_att) @ w_out


def setup_inputs(seed: int = 0) -> dict:
    key = jax.random.key(seed)
    ks = iter(jax.random.split(key, 40))

    def nrm(shape, std):
        return jax.random.normal(next(ks), shape, jnp.float32) * std

    def gain(shape):
        return 1.0 + nrm(shape, 0.02)

    L, D = DEPTH, D_MODEL
    return {
        "x": nrm((BATCH, SEQ, D), 1.0),
        "p": nrm((L, BATCH, SEQ, PLE_DIM), 1.0),
        "ffn1_w_gate": nrm((L, D, D_FF), D ** -0.5),
        "ffn1_w_up": nrm((L, D, D_FF), D ** -0.5),
        "ffn1_w_down": nrm((L, D_FF, D), D_FF ** -0.5 * BETA),
        "ln1_g": gain((L, D)),
        "ln1_b": nrm((L, D), 0.02),
        "mix_w_in": nrm((L, D, IN_WIDTH), D ** -0.5),
        "mix_b_f": jax.random.uniform(next(ks), (L, N_HEADS), jnp.float32, 1.0, 4.0),
        "mix_b_gate": nrm((L, N_BRANCHES * D), 0.01),
        "pool_w_group": nrm((L, N_POOL_GROUPS, POOL_GROUP, POOL_GROUP), POOL_GROUP ** -0.5),
        "pool_scale": gain((L, POOL_WIDTH)),
        "pool_w_up": nrm((L, POOL_WIDTH, D), POOL_WIDTH ** -0.5),
        "att_w_up": nrm((L, ATT_WIDTH, D), ATT_WIDTH ** -0.5),
        "mix_w_out": nrm((L, D, D), D ** -0.5 * BETA),
        "ln2_g": gain((L, D)),
        "ln2_b": nrm((L, D), 0.02),
        "ffn2_w_gate": nrm((L, D, D_FF), D ** -0.5),
        "ffn2_w_up": nrm((L, D, D_FF), D ** -0.5),
        "ffn2_w_down": nrm((L, D_FF, D), D_FF ** -0.5 * BETA),
        "ln3_g": gain((L, D)),
        "ln3_b": nrm((L, D), 0.02),
        "ple_w_proj": nrm((L, PLE_DIM, D), PLE_DIM ** -0.5 * BETA),
        "ple_w_gate": nrm((L, D, D), D ** -0.5),
        "ln4_g": gain((L, D)),
        "ln4_b": nrm((L, D), 0.02),
    }


def reference(x, p, ffn1_w_gate, ffn1_w_up, ffn1_w_down, ln1_g, ln1_b,
              mix_w_in, mix_b_f, mix_b_gate, pool_w_group, pool_scale, pool_w_up,
              att_w_up, mix_w_out, ln2_g, ln2_b,
              ffn2_w_gate, ffn2_w_up, ffn2_w_down, ln3_g, ln3_b,
              ple_w_proj, ple_w_gate, ln4_g, ln4_b):
    for i in range(DEPTH):
        x = layer_norm(ALPHA * x + 0.5 * swiglu(x, ffn1_w_gate[i], ffn1_w_up[i], ffn1_w_down[i]),
                       ln1_g[i], ln1_b[i])
        x = layer_norm(ALPHA * x + mixer(x, mix_w_in[i], mix_b_f[i], mix_b_gate[i],
                                         pool_w_group[i], pool_scale[i], pool_w_up[i],
                                         att_w_up[i], mix_w_out[i]),
                       ln2_g[i], ln2_b[i])
        x = layer_norm(ALPHA * x + 0.5 * swiglu(x, ffn2_w_gate[i], ffn2_w_up[i], ffn2_w_down[i]),
                       ln3_g[i], ln3_b[i])
        e = (p[i] @ ple_w_proj[i]) * jax.nn.sigmoid(x @ ple_w_gate[i])
        x = layer_norm(ALPHA * x + e, ln4_g[i], ln4_b[i])
    return x
```

```python
import functools

import jax
import jax.numpy as jnp
import numpy as np
from jax import lax
from jax.experimental import pallas as pl
from jax.experimental.pallas import tpu as pltpu

F32 = jnp.float32
BF16 = jnp.bfloat16

LN_EPS = 1e-5
POOL_WINDOWS = (2, 4, 8, 16)
HEAD_DIM = 64
LANES = 128
POOL_HALO = 16
AUG_W = 6
V7X_VMEM_BYTES = 64 * 1024 * 1024
VMEM_LIMIT = V7X_VMEM_BYTES * 7 // 8

FFN_CHUNK = 256
TM_FFN = 512
TM_MIX = 512
TQ = 512


def _layer_norm(y, g, b):
    mu = jnp.mean(y, axis=-1, keepdims=True)
    yc = y - mu
    var = jnp.mean(yc * yc, axis=-1, keepdims=True)
    return yc * lax.rsqrt(var + LN_EPS) * g + b


def _sigmoid(x):
    return 1.0 / (1.0 + jnp.exp(-x))


def _const_spec(shape):
    nd = len(shape)
    return pl.BlockSpec(shape, lambda *_: (0,) * nd, pipeline_mode=pl.Buffered(1))


def _ffn_kernel(*refs, alpha, n_chunks, with_ple):
    if with_ple:
        (x_ref, wgu_ref, wd_ref, g_ref, b_ref, p_ref, wp_ref, wpg_ref, g2_ref, b2_ref,
         o_ref, h_ref) = refs
    else:
        x_ref, wgu_ref, wd_ref, g_ref, b_ref, o_ref, h_ref = refs
    fc = FFN_CHUNK
    x = x_ref[...]
    xb = x.astype(BF16)
    for c in range(n_chunks):
        gu = jnp.dot(xb, wgu_ref[:, 2 * c * fc:2 * (c + 1) * fc], preferred_element_type=F32)
        g = gu[:, :fc]
        u = gu[:, fc:]
        h_ref[:, c * fc:(c + 1) * fc] = (g * _sigmoid(g) * u).astype(BF16)
    y = jnp.dot(h_ref[...], wd_ref[...], preferred_element_type=F32)
    x = _layer_norm(alpha * x + 0.5 * y, g_ref[...], b_ref[...])
    if with_ple:
        proj = jnp.dot(p_ref[...].astype(BF16), wp_ref[...], preferred_element_type=F32)
        gate = _sigmoid(jnp.dot(x.astype(BF16), wpg_ref[...], preferred_element_type=F32))
        x = _layer_norm(alpha * x + proj * gate, g2_ref[...], b2_ref[...])
    o_ref[...] = x


def _ffn_ln(x, wgu, wd, g, b, alpha, ple=None):
    m, d = x.shape
    dff = wd.shape[0]
    tm = TM_FFN
    row = lambda i: (i, 0)
    in_specs = [pl.BlockSpec((tm, d), row), _const_spec(wgu.shape), _const_spec(wd.shape),
                _const_spec(g.shape), _const_spec(b.shape)]
    args = [x, wgu, wd, g, b]
    if ple is not None:
        p, wp, wpg, g2, b2 = ple
        in_specs += [pl.BlockSpec((tm, p.shape[1]), row), _const_spec(wp.shape), _const_spec(wpg.shape),
                     _const_spec(g2.shape), _const_spec(b2.shape)]
        args += [p, wp, wpg, g2, b2]
    return pl.pallas_call(
        functools.partial(_ffn_kernel, alpha=alpha, n_chunks=dff // FFN_CHUNK, with_ple=ple is not None),
        grid=(m // tm,),
        in_specs=in_specs,
        out_specs=pl.BlockSpec((tm, d), row),
        out_shape=jax.ShapeDtypeStruct((m, d), F32),
        scratch_shapes=[pltpu.VMEM((tm, dff), BF16)],
        compiler_params=pltpu.CompilerParams(dimension_semantics=("arbitrary",),
                                             vmem_limit_bytes=VMEM_LIMIT),
        name="ffn_ln_ple" if ple is not None else "ffn_ln",
    )(*args)


def _split3(x):
    hi = x.astype(BF16).astype(F32)
    r = x - hi
    mid = r.astype(BF16).astype(F32)
    lo = (r - mid).astype(BF16).astype(F32)
    return hi, mid, lo


def _mix_in_kernel(x_ref, w_ref, wf_ref, bf_ref, wbd_ref, pscale_ref, masks_ref,
                   qa_ref, ka_ref, va_ref, yp_ref, uext_ref, fcarry_ref, *, tm, n_heads, pool_w):
    si = pl.program_id(1)

    @pl.when(si == 0)
    def _():
        uext_ref[0:POOL_HALO, :] = jnp.zeros((POOL_HALO, pool_w), F32)
        fcarry_ref[...] = jnp.zeros_like(fcarry_ref)

    xb = x_ref[...].astype(BF16)
    z = jnp.dot(xb, w_ref[...], preferred_element_type=F32)

    u = z[:, :pool_w]
    uext_ref[POOL_HALO:POOL_HALO + tm, :] = u
    t_glob = si * tm + lax.broadcasted_iota(jnp.int32, (tm, 1), 0)
    pg = pool_w // len(POOL_WINDOWS)
    rs = []
    for gi, w in enumerate(POOL_WINDOWS):
        lanes = slice(gi * pg, (gi + 1) * pg)
        ug = u[:, lanes]
        acc = ug
        for j in range(1, w):
            acc = acc + uext_ref[POOL_HALO - j:POOL_HALO - j + tm, lanes]
        cnt = jnp.minimum(t_glob + 1, w).astype(F32)
        rs.append(acc / cnt - ug)
    r = jnp.concatenate(rs, axis=1).astype(BF16)
    y = jnp.dot(r, wbd_ref[...], preferred_element_type=F32) * pscale_ref[...]
    yp_ref[...] = y.astype(BF16)
    uext_ref[0:POOL_HALO, :] = uext_ref[tm:tm + POOL_HALO, :]

    fl = jnp.dot(xb, wf_ref[...], preferred_element_type=F32) + bf_ref[...]
    logf = jnp.minimum(fl, 0.0) - jnp.log1p(jnp.exp(-jnp.abs(fl)))
    tri = (lax.broadcasted_iota(jnp.int32, (tm, tm), 0)
           >= lax.broadcasted_iota(jnp.int32, (tm, tm), 1)).astype(BF16)
    a, b, c = _split3(logf)
    fcum = (jnp.dot(tri, a.astype(BF16), preferred_element_type=F32)
            + jnp.dot(tri, b.astype(BF16), preferred_element_type=F32)
            + jnp.dot(tri, c.astype(BF16), preferred_element_type=F32)) + fcarry_ref[...]
    fcarry_ref[...] = fcum[tm - 1:tm, :]

    hi, mid, lo = _split3(fcum)
    mk = masks_ref[...]
    aq = hi * mk[0:1] + mid * mk[1:2] + lo * mk[2:3] + (mk[3:4] + mk[4:5] + mk[5:6])
    ak = (mk[0:1] + mk[1:2] + mk[2:3]) - (hi * mk[3:4] + mid * mk[4:5] + lo * mk[5:6])
    att_w = n_heads * HEAD_DIM
    scale = HEAD_DIM ** -0.5
    for h in range(n_heads):
        j, e = divmod(h, 2)
        dm = mk[6 + e:7 + e]
        am = mk[8 + h:9 + h]
        zq = z[:, pool_w + j * LANES:pool_w + (j + 1) * LANES]
        zk = z[:, pool_w + att_w + j * LANES:pool_w + att_w + (j + 1) * LANES]
        zv = z[:, pool_w + 2 * att_w + j * LANES:pool_w + 2 * att_w + (j + 1) * LANES]
        qa_ref[h] = (zq * (scale * dm) + aq * am).astype(BF16)
        ka_ref[h] = (zk * dm + ak * am).astype(BF16)
        va_ref[h] = (zv * dm + (1.0 - dm)).astype(BF16)


def _aug_lane_tables(n_heads):
    half = LANES // 2
    masks = np.zeros((8 + n_heads, LANES), np.float32)
    head_of_lane = np.full((LANES,), -1, np.int64)
    masks[6, :half] = 1.0
    masks[7, half:] = 1.0
    for h in range(n_heads):
        j, e = divmod(h, 2)
        base = half * (1 - e) + AUG_W * j
        for pos in range(AUG_W):
            masks[pos, base + pos] = 1.0
            masks[8 + h, base + pos] = 1.0
            head_of_lane[base + pos] = h
    return masks, head_of_lane


def _mix_in(x1, w_uqkv, wf_rep, bf_rep, wbd, pscale, masks, n_heads):
    bsz, seq, d = x1.shape
    tm = TM_MIX
    pool_w = wbd.shape[0]
    hspec = pl.BlockSpec((None, n_heads, tm, LANES), lambda b, s: (b, 0, s, 0))
    hshape = jax.ShapeDtypeStruct((bsz, n_heads, seq, LANES), BF16)
    return pl.pallas_call(
        functools.partial(_mix_in_kernel, tm=tm, n_heads=n_heads, pool_w=pool_w),
        grid=(bsz, seq // tm),
        in_specs=[pl.BlockSpec((None, tm, d), lambda b, s: (b, s, 0)),
                  _const_spec(w_uqkv.shape), _const_spec(wf_rep.shape), _const_spec(bf_rep.shape),
                  _const_spec(wbd.shape), _const_spec(pscale.shape), _const_spec(masks.shape)],
        out_specs=[hspec, hspec, hspec,
                   pl.BlockSpec((None, tm, pool_w), lambda b, s: (b, s, 0))],
        out_shape=[hshape, hshape, hshape, jax.ShapeDtypeStruct((bsz, seq, pool_w), BF16)],
        scratch_shapes=[pltpu.VMEM((POOL_HALO + tm, pool_w), F32), pltpu.VMEM((1, LANES), F32)],
        compiler_params=pltpu.CompilerParams(dimension_semantics=("arbitrary", "arbitrary"),
                                             vmem_limit_bytes=VMEM_LIMIT),
        name="mix_in",
    )(x1, w_uqkv, wf_rep, bf_rep, wbd, pscale, masks)


def _attn_kernel(qa_ref, ka_ref, va_ref, o_ref, *, tq):
    qi = pl.program_id(2)
    nt = (((1,), (1,)), ((), ()))
    accs = []
    for e in range(2):
        q = qa_ref[e]
        d0 = pl.multiple_of(qi * tq, tq)
        s = lax.dot_general(q, ka_ref[e, pl.ds(d0, tq), :], nt, preferred_element_type=F32)
        causal = (lax.broadcasted_iota(jnp.int32, (tq, tq), 0)
                  >= lax.broadcasted_iota(jnp.int32, (tq, tq), 1))
        s = jnp.where(causal, s, -jnp.inf)
        m = jnp.max(s, axis=1, keepdims=True)
        p = jnp.exp(s - m)
        acc = jnp.dot(p.astype(BF16), va_ref[e, pl.ds(d0, tq), :], preferred_element_type=F32)

        def body(kb, carry, e=e, q=q):
            m, acc = carry
            k0 = pl.multiple_of(kb * tq, tq)
            s = lax.dot_general(q, ka_ref[e, pl.ds(k0, tq), :], nt, preferred_element_type=F32)
            m_new = jnp.maximum(m, jnp.max(s, axis=1, keepdims=True))
            p = jnp.exp(s - m_new)
            acc = jnp.exp(m - m_new) * acc + jnp.dot(p.astype(BF16), va_ref[e, pl.ds(k0, tq), :],
                                                     preferred_element_type=F32)
            return m_new, acc

        m, acc = lax.fori_loop(0, qi, body, (m, acc))
        accs.append(acc)
    half = LANES // 2
    lane = lax.broadcasted_iota(jnp.int32, (tq, LANES), 1)
    lo_half = lane < half
    o_even = accs[0] / jnp.where(lo_half, pltpu.roll(accs[0], half, 1), 1.0)
    o_odd = accs[1] / jnp.where(lo_half, 1.0, pltpu.roll(accs[1], half, 1))
    o_ref[...] = jnp.where(lo_half, o_even, o_odd).astype(BF16)


def _attention(qa, ka, va):
    bsz, n_heads, seq, _ = qa.shape
    tq = TQ
    return pl.pallas_call(
        functools.partial(_attn_kernel, tq=tq),
        grid=(bsz, n_heads // 2, seq // tq),
        in_specs=[pl.BlockSpec((None, 2, tq, LANES), lambda b, j, i: (b, j, i, 0)),
                  pl.BlockSpec((None, 2, seq, LANES), lambda b, j, i: (b, j, 0, 0)),
                  pl.BlockSpec((None, 2, seq, LANES), lambda b, j, i: (b, j, 0, 0))],
        out_specs=pl.BlockSpec((None, tq, LANES), lambda b, j, i: (b, i, j)),
        out_shape=jax.ShapeDtypeStruct((bsz, seq, n_heads * HEAD_DIM), BF16),
        compiler_params=pltpu.CompilerParams(dimension_semantics=("arbitrary",) * 3,
                                             vmem_limit_bytes=VMEM_LIMIT),
        name="fox_attention",
    )(qa, ka, va)


def _mix_out_kernel(x_ref, o_ref, yp_ref, wgl_ref, bg_ref, wpu_ref, wau_ref, wo_ref, g_ref, b_ref,
                    out_ref, *, alpha):
    x = x_ref[...]
    d = x.shape[1]
    gates = _sigmoid(jnp.dot(x.astype(BF16), wgl_ref[...], preferred_element_type=F32) + bg_ref[...])
    y_pool = jnp.dot(yp_ref[...], wpu_ref[...], preferred_element_type=F32)
    y_att = jnp.dot(o_ref[...], wau_ref[...], preferred_element_type=F32)
    merged = gates[:, :d] * y_pool + gates[:, d:] * y_att
    y = jnp.dot(merged.astype(BF16), wo_ref[...], preferred_element_type=F32)
    out_ref[...] = _layer_norm(alpha * x + y, g_ref[...], b_ref[...])


def _mix_out(x1, o, yp, wgl, bg, wpu, wau, wo, g, b, alpha):
    m, d = x1.shape
    tm = TM_FFN
    row = lambda i: (i, 0)
    consts = [wgl, bg, wpu, wau, wo, g, b]
    return pl.pallas_call(
        functools.partial(_mix_out_kernel, alpha=alpha),
        grid=(m // tm,),
        in_specs=[pl.BlockSpec((tm, d), row), pl.BlockSpec((tm, o.shape[1]), row),
                  pl.BlockSpec((tm, yp.shape[1]), row)] + [_const_spec(c.shape) for c in consts],
        out_specs=pl.BlockSpec((tm, d), row),
        out_shape=jax.ShapeDtypeStruct((m, d), F32),
        compiler_params=pltpu.CompilerParams(dimension_semantics=("arbitrary",),
                                             vmem_limit_bytes=VMEM_LIMIT),
        name="mix_out",
    )(x1, o, yp, *consts)


def _interleave_gate_up(w_gate, w_up):
    d, dff = w_gate.shape
    n = dff // FFN_CHUNK
    w = jnp.stack([w_gate.reshape(d, n, FFN_CHUNK), w_up.reshape(d, n, FFN_CHUNK)], axis=2)
    return w.reshape(d, 2 * dff).astype(BF16)


def _block_diag(w_group):
    n, c, _ = w_group.shape
    out = jnp.zeros((n * c, n * c), w_group.dtype)
    for gi in range(n):
        out = out.at[gi * c:(gi + 1) * c, gi * c:(gi + 1) * c].set(w_group[gi])
    return out


def kernel(x, p, ffn1_w_gate, ffn1_w_up, ffn1_w_down, ln1_g, ln1_b, mix_w_in, mix_b_f, mix_b_gate, pool_w_group, pool_scale, pool_w_up, att_w_up, mix_w_out, ln2_g, ln2_b, ffn2_w_gate, ffn2_w_up, ffn2_w_down, ln3_g, ln3_b, ple_w_proj, ple_w_gate, ln4_g, ln4_b):
    bsz, seq, d = x.shape
    depth = p.shape[0]
    n_heads = mix_b_f.shape[1]
    pool_w = pool_scale.shape[1]
    att_w = n_heads * HEAD_DIM
    alpha = (2 * depth) ** 0.25
    masks_np, head_of_lane = _aug_lane_tables(n_heads)
    masks = jnp.asarray(masks_np)
    lane_sel = jnp.asarray(np.maximum(head_of_lane, 0))
    lane_on = jnp.asarray((head_of_lane >= 0).astype(np.float32))
    row2 = lambda v: v.reshape(1, -1)

    xf = x.reshape(bsz * seq, d)
    for i in range(depth):
        xf = _ffn_ln(xf, _interleave_gate_up(ffn1_w_gate[i], ffn1_w_up[i]), ffn1_w_down[i].astype(BF16),
                     row2(ln1_g[i]), row2(ln1_b[i]), alpha)

        w_in = mix_w_in[i]
        qkv_end = pool_w + 3 * att_w
        w_f = w_in[:, qkv_end:qkv_end + n_heads]
        wf_rep = (w_f[:, lane_sel] * lane_on).astype(BF16)
        bf_rep = row2(mix_b_f[i][lane_sel] * lane_on)
        qa, ka, va, yp = _mix_in(xf.reshape(bsz, seq, d), w_in[:, :qkv_end].astype(BF16), wf_rep, bf_rep,
                                 _block_diag(pool_w_group[i]).astype(BF16), row2(pool_scale[i]), masks, n_heads)
        o = _attention(qa, ka, va)
        xf = _mix_out(xf, o.reshape(bsz * seq, att_w), yp.reshape(bsz * seq, pool_w),
                      w_in[:, qkv_end + n_heads:].astype(BF16), row2(mix_b_gate[i]),
                      pool_w_up[i].astype(BF16), att_w_up[i].astype(BF16), mix_w_out[i].astype(BF16),
                      row2(ln2_g[i]), row2(ln2_b[i]), alpha)

        xf = _ffn_ln(xf, _interleave_gate_up(ffn2_w_gate[i], ffn2_w_up[i]), ffn2_w_down[i].astype(BF16),
                     row2(ln3_g[i]), row2(ln3_b[i]), alpha,
                     ple=(p[i].reshape(bsz * seq, -1), ple_w_proj[i].astype(BF16), ple_w_gate[i].astype(BF16),
                          row2(ln4_g[i]), row2(ln4_b[i])))
    return xf.reshape(bsz, seq, d)
```

```python
import functools

import jax
import jax.numpy as jnp
import numpy as np
from jax import lax
from jax.experimental import pallas as pl
from jax.experimental.pallas import tpu as pltpu

F32 = jnp.float32
BF16 = jnp.bfloat16

LN_EPS = 1e-5
POOL_WINDOWS = (2, 4, 8, 16)
HEAD_DIM = 64
LANES = 128
POOL_HALO = 16
AUG_W = 6
V7X_VMEM_BYTES = 64 * 1024 * 1024
VMEM_LIMIT = V7X_VMEM_BYTES * 7 // 8

FFN_CHUNK = 256
TM_FFN = 512
TM_MIX = 512
TQ = 512
HEADS_PER_STEP = 8


def _layer_norm(y, g, b):
    mu = jnp.mean(y, axis=-1, keepdims=True)
    yc = y - mu
    var = jnp.mean(yc * yc, axis=-1, keepdims=True)
    return yc * lax.rsqrt(var + LN_EPS) * g + b


def _sigmoid(x):
    return 1.0 / (1.0 + jnp.exp(-x))


def _const_spec(shape):
    nd = len(shape)
    return pl.BlockSpec(shape, lambda *_: (0,) * nd, pipeline_mode=pl.Buffered(1))


def _ffn_kernel(*refs, alpha, n_chunks, with_ple):
    if with_ple:
        (x_ref, wg_ref, wu_ref, wd_ref, g_ref, b_ref, p_ref, wp_ref, wpg_ref, g2_ref, b2_ref,
         o_ref, h_ref) = refs
    else:
        x_ref, wg_ref, wu_ref, wd_ref, g_ref, b_ref, o_ref, h_ref = refs
    fc = FFN_CHUNK
    x = x_ref[...]
    xb = x.astype(BF16)
    for c in range(n_chunks):
        cols = slice(c * fc, (c + 1) * fc)
        g = jnp.dot(xb, wg_ref[:, cols], preferred_element_type=F32)
        u = jnp.dot(xb, wu_ref[:, cols], preferred_element_type=F32)
        h_ref[:, cols] = (g * _sigmoid(g) * u).astype(BF16)
    y = jnp.dot(h_ref[...], wd_ref[...], preferred_element_type=F32)
    x = _layer_norm(alpha * x + 0.5 * y, g_ref[...], b_ref[...])
    if with_ple:
        proj = jnp.dot(p_ref[...].astype(BF16), wp_ref[...], preferred_element_type=F32)
        gate = _sigmoid(jnp.dot(x.astype(BF16), wpg_ref[...], preferred_element_type=F32))
        x = _layer_norm(alpha * x + proj * gate, g2_ref[...], b2_ref[...])
    o_ref[...] = x


def _ffn_ln(x, wg, wu, wd, g, b, alpha, ple=None):
    m, d = x.shape
    dff = wd.shape[0]
    tm = TM_FFN
    row = lambda i: (i, 0)
    in_specs = [pl.BlockSpec((tm, d), row), _const_spec(wg.shape), _const_spec(wu.shape), _const_spec(wd.shape),
                _const_spec(g.shape), _const_spec(b.shape)]
    args = [x, wg, wu, wd, g, b]
    if ple is not None:
        p, wp, wpg, g2, b2 = ple
        in_specs += [pl.BlockSpec((tm, p.shape[1]), row), _const_spec(wp.shape), _const_spec(wpg.shape),
                     _const_spec(g2.shape), _const_spec(b2.shape)]
        args += [p, wp, wpg, g2, b2]
    return pl.pallas_call(
        functools.partial(_ffn_kernel, alpha=alpha, n_chunks=dff // FFN_CHUNK, with_ple=ple is not None),
        grid=(m // tm,),
        in_specs=in_specs,
        out_specs=pl.BlockSpec((tm, d), row),
        out_shape=jax.ShapeDtypeStruct((m, d), F32),
        scratch_shapes=[pltpu.VMEM((tm, dff), BF16)],
        compiler_params=pltpu.CompilerParams(dimension_semantics=("arbitrary",),
                                             vmem_limit_bytes=VMEM_LIMIT),
        name="ffn_ln_ple" if ple is not None else "ffn_ln",
    )(*args)


def _split3(x):
    hi = x.astype(BF16).astype(F32)
    r = x - hi
    mid = r.astype(BF16).astype(F32)
    lo = (r - mid).astype(BF16).astype(F32)
    return hi, mid, lo


def _mix_in_kernel(x_ref, w_ref, wf_ref, bf_ref, wbd_ref, pscale_ref, masks_ref,
                   qa_ref, ka_ref, va_ref, yp_ref, uext_ref, fcarry_ref, *, tm, n_heads, pool_w):
    si = pl.program_id(1)

    @pl.when(si == 0)
    def _():
        uext_ref[0:POOL_HALO, :] = jnp.zeros((POOL_HALO, pool_w), F32)
        fcarry_ref[...] = jnp.zeros_like(fcarry_ref)

    xb = x_ref[...].astype(BF16)
    z = jnp.dot(xb, w_ref[...], preferred_element_type=F32)

    u = z[:, :pool_w]
    uext_ref[POOL_HALO:POOL_HALO + tm, :] = u
    t_glob = si * tm + lax.broadcasted_iota(jnp.int32, (tm, 1), 0)
    pg = pool_w // len(POOL_WINDOWS)
    rs = []
    for gi, w in enumerate(POOL_WINDOWS):
        lanes = slice(gi * pg, (gi + 1) * pg)
        ug = u[:, lanes]
        acc = ug
        for j in range(1, w):
            acc = acc + uext_ref[POOL_HALO - j:POOL_HALO - j + tm, lanes]
        cnt = jnp.minimum(t_glob + 1, w).astype(F32)
        rs.append(acc / cnt - ug)
    r = jnp.concatenate(rs, axis=1).astype(BF16)
    y = jnp.dot(r, wbd_ref[...], preferred_element_type=F32) * pscale_ref[...]
    yp_ref[...] = y.astype(BF16)
    uext_ref[0:POOL_HALO, :] = uext_ref[tm:tm + POOL_HALO, :]

    fl = jnp.dot(xb, wf_ref[...], preferred_element_type=F32) + bf_ref[...]
    logf = jnp.minimum(fl, 0.0) - jnp.log1p(jnp.exp(-jnp.abs(fl)))
    tri = (lax.broadcasted_iota(jnp.int32, (tm, tm), 0)
           >= lax.broadcasted_iota(jnp.int32, (tm, tm), 1)).astype(BF16)
    a, b, c = _split3(logf)
    fcum = (jnp.dot(tri, a.astype(BF16), preferred_element_type=F32)
            + jnp.dot(tri, b.astype(BF16), preferred_element_type=F32)
            + jnp.dot(tri, c.astype(BF16), preferred_element_type=F32)) + fcarry_ref[...]
    fcarry_ref[...] = fcum[tm - 1:tm, :]

    hi, mid, lo = _split3(fcum)
    mk = masks_ref[...]
    aq = hi * mk[0:1] + mid * mk[1:2] + lo * mk[2:3] + (mk[3:4] + mk[4:5] + mk[5:6])
    ak = (mk[0:1] + mk[1:2] + mk[2:3]) - (hi * mk[3:4] + mid * mk[4:5] + lo * mk[5:6])
    att_w = n_heads * HEAD_DIM
    scale = HEAD_DIM ** -0.5
    for h in range(n_heads):
        j, e = divmod(h, 2)
        dm = mk[6 + e:7 + e]
        am = mk[8 + h:9 + h]
        zq = z[:, pool_w + j * LANES:pool_w + (j + 1) * LANES]
        zk = z[:, pool_w + att_w + j * LANES:pool_w + att_w + (j + 1) * LANES]
        zv = z[:, pool_w + 2 * att_w + j * LANES:pool_w + 2 * att_w + (j + 1) * LANES]
        qa_ref[h] = (zq * (scale * dm) + aq * am).astype(BF16)
        ka_ref[h] = (zk * dm + ak * am).astype(BF16)
        va_ref[h] = (zv * dm + (1.0 - dm)).astype(BF16)


def _aug_lane_tables(n_heads):
    half = LANES // 2
    masks = np.zeros((8 + n_heads, LANES), np.float32)
    head_of_lane = np.full((LANES,), -1, np.int64)
    masks[6, :half] = 1.0
    masks[7, half:] = 1.0
    for h in range(n_heads):
        j, e = divmod(h, 2)
        base = half * (1 - e) + AUG_W * j
        for pos in range(AUG_W):
            masks[pos, base + pos] = 1.0
            masks[8 + h, base + pos] = 1.0
            head_of_lane[base + pos] = h
    return masks, head_of_lane


def _mix_in(x1, w_uqkv, wf_rep, bf_rep, wbd, pscale, masks, n_heads):
    bsz, seq, d = x1.shape
    tm = TM_MIX
    pool_w = wbd.shape[0]
    hspec = pl.BlockSpec((None, n_heads, tm, LANES), lambda b, s: (b, 0, s, 0))
    hshape = jax.ShapeDtypeStruct((bsz, n_heads, seq, LANES), BF16)
    return pl.pallas_call(
        functools.partial(_mix_in_kernel, tm=tm, n_heads=n_heads, pool_w=pool_w),
        grid=(bsz, seq // tm),
        in_specs=[pl.BlockSpec((None, tm, d), lambda b, s: (b, s, 0)),
                  _const_spec(w_uqkv.shape), _const_spec(wf_rep.shape), _const_spec(bf_rep.shape),
                  _const_spec(wbd.shape), _const_spec(pscale.shape), _const_spec(masks.shape)],
        out_specs=[hspec, hspec, hspec,
                   pl.BlockSpec((None, tm, pool_w), lambda b, s: (b, s, 0))],
        out_shape=[hshape, hshape, hshape, jax.ShapeDtypeStruct((bsz, seq, pool_w), BF16)],
        scratch_shapes=[pltpu.VMEM((POOL_HALO + tm, pool_w), F32), pltpu.VMEM((1, LANES), F32)],
        compiler_params=pltpu.CompilerParams(dimension_semantics=("arbitrary", "arbitrary"),
                                             vmem_limit_bytes=VMEM_LIMIT),
        name="mix_in",
    )(x1, w_uqkv, wf_rep, bf_rep, wbd, pscale, masks)


def _attn_kernel(qa_ref, ka_ref, va_ref, o_ref, *, tq, hps):
    qi = pl.program_id(2)
    nt = (((1,), (1,)), ((), ()))
    qs = [qa_ref[e] for e in range(hps)]

    def scores(e, k0):
        return lax.dot_general(qs[e], ka_ref[e, pl.ds(k0, tq), :], nt, preferred_element_type=F32)

    def weighted_values(e, k0, p):
        return jnp.dot(p.astype(BF16), va_ref[e, pl.ds(k0, tq), :], preferred_element_type=F32)

    d0 = pl.multiple_of(qi * tq, tq)
    causal = (lax.broadcasted_iota(jnp.int32, (tq, tq), 0)
              >= lax.broadcasted_iota(jnp.int32, (tq, tq), 1))
    ms, accs = [], []
    for e in range(hps):
        s = jnp.where(causal, scores(e, d0), -jnp.inf)
        m = jnp.max(s, axis=1, keepdims=True)
        ms.append(m)
        accs.append(weighted_values(e, d0, jnp.exp(s - m)))

    def body(kb, carry):
        ms, accs = carry
        k0 = pl.multiple_of(kb * tq, tq)
        new_ms, new_accs = [], []
        for e in range(hps):
            s = scores(e, k0)
            m_new = jnp.maximum(ms[e], jnp.max(s, axis=1, keepdims=True))
            new_accs.append(jnp.exp(ms[e] - m_new) * accs[e] + weighted_values(e, k0, jnp.exp(s - m_new)))
            new_ms.append(m_new)
        return tuple(new_ms), tuple(new_accs)

    ms, accs = lax.fori_loop(0, qi, body, (tuple(ms), tuple(accs)))

    half = LANES // 2
    lo_half = lax.broadcasted_iota(jnp.int32, (tq, LANES), 1) < half
    outs = []
    for j in range(hps // 2):
        even, odd = accs[2 * j], accs[2 * j + 1]
        o_even = even / jnp.where(lo_half, pltpu.roll(even, half, 1), 1.0)
        o_odd = odd / jnp.where(lo_half, 1.0, pltpu.roll(odd, half, 1))
        outs.append(jnp.where(lo_half, o_even, o_odd))
    o_ref[...] = jnp.concatenate(outs, axis=1).astype(BF16)


def _attention(qa, ka, va):
    bsz, n_heads, seq, _ = qa.shape
    tq, hps = TQ, HEADS_PER_STEP
    return pl.pallas_call(
        functools.partial(_attn_kernel, tq=tq, hps=hps),
        grid=(bsz, n_heads // hps, seq // tq),
        in_specs=[pl.BlockSpec((None, hps, tq, LANES), lambda b, j, i: (b, j, i, 0)),
                  pl.BlockSpec((None, hps, seq, LANES), lambda b, j, i: (b, j, 0, 0)),
                  pl.BlockSpec((None, hps, seq, LANES), lambda b, j, i: (b, j, 0, 0))],
        out_specs=pl.BlockSpec((None, tq, hps * HEAD_DIM), lambda b, j, i: (b, i, j)),
        out_shape=jax.ShapeDtypeStruct((bsz, seq, n_heads * HEAD_DIM), BF16),
        compiler_params=pltpu.CompilerParams(dimension_semantics=("arbitrary",) * 3,
                                             vmem_limit_bytes=VMEM_LIMIT),
        name="fox_attention",
    )(qa, ka, va)


def _mix_out_kernel(x_ref, o_ref, yp_ref, wgl_ref, bg_ref, wpu_ref, wau_ref, wo_ref, g_ref, b_ref,
                    out_ref, *, alpha):
    x = x_ref[...]
    d = x.shape[1]
    gates = _sigmoid(jnp.dot(x.astype(BF16), wgl_ref[...], preferred_element_type=F32) + bg_ref[...])
    y_pool = jnp.dot(yp_ref[...], wpu_ref[...], preferred_element_type=F32)
    y_att = jnp.dot(o_ref[...], wau_ref[...], preferred_element_type=F32)
    merged = gates[:, :d] * y_pool + gates[:, d:] * y_att
    y = jnp.dot(merged.astype(BF16), wo_ref[...], preferred_element_type=F32)
    out_ref[...] = _layer_norm(alpha * x + y, g_ref[...], b_ref[...])


def _mix_out(x1, o, yp, wgl, bg, wpu, wau, wo, g, b, alpha):
    m, d = x1.shape
    tm = TM_FFN
    row = lambda i: (i, 0)
    consts = [wgl, bg, wpu, wau, wo, g, b]
    return pl.pallas_call(
        functools.partial(_mix_out_kernel, alpha=alpha),
        grid=(m // tm,),
        in_specs=[pl.BlockSpec((tm, d), row), pl.BlockSpec((tm, o.shape[1]), row),
                  pl.BlockSpec((tm, yp.shape[1]), row)] + [_const_spec(c.shape) for c in consts],
        out_specs=pl.BlockSpec((tm, d), row),
        out_shape=jax.ShapeDtypeStruct((m, d), F32),
        compiler_params=pltpu.CompilerParams(dimension_semantics=("arbitrary",),
                                             vmem_limit_bytes=VMEM_LIMIT),
        name="mix_out",
    )(x1, o, yp, *consts)


def _block_diag(w_group):
    n, c, _ = w_group.shape
    out = jnp.zeros((n * c, n * c), w_group.dtype)
    for gi in range(n):
        out = out.at[gi * c:(gi + 1) * c, gi * c:(gi + 1) * c].set(w_group[gi])
    return out


def kernel(x, p, ffn1_w_gate, ffn1_w_up, ffn1_w_down, ln1_g, ln1_b, mix_w_in, mix_b_f, mix_b_gate, pool_w_group, pool_scale, pool_w_up, att_w_up, mix_w_out, ln2_g, ln2_b, ffn2_w_gate, ffn2_w_up, ffn2_w_down, ln3_g, ln3_b, ple_w_proj, ple_w_gate, ln4_g, ln4_b):
    bsz, seq, d = x.shape
    depth = p.shape[0]
    n_heads = mix_b_f.shape[1]
    pool_w = pool_scale.shape[1]
    att_w = n_heads * HEAD_DIM
    alpha = (2 * depth) ** 0.25
    masks_np, head_of_lane = _aug_lane_tables(n_heads)
    masks = jnp.asarray(masks_np)
    lane_sel = jnp.asarray(np.maximum(head_of_lane, 0))
    lane_on = jnp.asarray((head_of_lane >= 0).astype(np.float32))
    row2 = lambda v: v.reshape(1, -1)

    xf = x.reshape(bsz * seq, d)
    for i in range(depth):
        xf = _ffn_ln(xf, ffn1_w_gate[i].astype(BF16), ffn1_w_up[i].astype(BF16), ffn1_w_down[i].astype(BF16),
                     row2(ln1_g[i]), row2(ln1_b[i]), alpha)

        w_in = mix_w_in[i]
        qkv_end = pool_w + 3 * att_w
        w_f = w_in[:, qkv_end:qkv_end + n_heads]
        wf_rep = (w_f[:, lane_sel] * lane_on).astype(BF16)
        bf_rep = row2(mix_b_f[i][lane_sel] * lane_on)
        qa, ka, va, yp = _mix_in(xf.reshape(bsz, seq, d), w_in[:, :qkv_end].astype(BF16), wf_rep, bf_rep,
                                 _block_diag(pool_w_group[i]).astype(BF16), row2(pool_scale[i]), masks, n_heads)
        o = _attention(qa, ka, va)
        xf = _mix_out(xf, o.reshape(bsz * seq, att_w), yp.reshape(bsz * seq, pool_w),
                      w_in[:, qkv_end + n_heads:].astype(BF16), row2(mix_b_gate[i]),
                      pool_w_up[i].astype(BF16), att_w_up[i].astype(BF16), mix_w_out[i].astype(BF16),
                      row2(ln2_g[i]), row2(ln2_b[i]), alpha)

        xf = _ffn_ln(xf, ffn2_w_gate[i].astype(BF16), ffn2_w_up[i].astype(BF16), ffn2_w_down[i].astype(BF16),
                     row2(ln3_g[i]), row2(ln3_b[i]), alpha,
                     ple=(p[i].reshape(bsz * seq, -1), ple_w_proj[i].astype(BF16), ple_w_gate[i].astype(BF16),
                          row2(ln4_g[i]), row2(ln4_b[i])))
    return xf.reshape(bsz, seq, d)
```

```python
import functools

import jax
import jax.numpy as jnp
import numpy as np
from jax import lax
from jax.experimental import pallas as pl
from jax.experimental.pallas import tpu as pltpu

F32 = jnp.float32
BF16 = jnp.bfloat16

LN_EPS = 1e-5
POOL_WINDOWS = (2, 4, 8, 16)
HEAD_DIM = 64
LANES = 128
POOL_HALO = 16
AUG_W = 6
V7X_VMEM_BYTES = 64 * 1024 * 1024
VMEM_LIMIT = V7X_VMEM_BYTES * 7 // 8

FFN_CHUNK = 256
TM_FFN = 512
TM_MIX = 512
LOG2_E = float(np.log2(np.e))
QK_LOOKAHEAD = 2
PV_LAG = 0
HEADS_PER_STEP = 8


def _layer_norm(y, g, b):
    mu = jnp.mean(y, axis=-1, keepdims=True)
    yc = y - mu
    var = jnp.mean(yc * yc, axis=-1, keepdims=True)
    return yc * lax.rsqrt(var + LN_EPS) * g + b


def _sigmoid(x):
    return 1.0 / (1.0 + jnp.exp(-x))


def _const_spec(shape):
    nd = len(shape)
    return pl.BlockSpec(shape, lambda *_: (0,) * nd, pipeline_mode=pl.Buffered(1))


def _ffn_kernel(*refs, alpha, n_chunks, with_ple):
    if with_ple:
        (x_ref, wg_ref, wu_ref, wd_ref, g_ref, b_ref, p_ref, wp_ref, wpg_ref, g2_ref, b2_ref,
         o_ref, h_ref) = refs
    else:
        x_ref, wg_ref, wu_ref, wd_ref, g_ref, b_ref, o_ref, h_ref = refs
    fc = FFN_CHUNK
    x = x_ref[...]
    xb = x.astype(BF16)
    for c in range(n_chunks):
        cols = slice(c * fc, (c + 1) * fc)
        g = jnp.dot(xb, wg_ref[:, cols], preferred_element_type=F32)
        u = jnp.dot(xb, wu_ref[:, cols], preferred_element_type=F32)
        h_ref[:, cols] = (g * _sigmoid(g) * u).astype(BF16)
    y = jnp.dot(h_ref[...], wd_ref[...], preferred_element_type=F32)
    x = _layer_norm(alpha * x + 0.5 * y, g_ref[...], b_ref[...])
    if with_ple:
        proj = jnp.dot(p_ref[...].astype(BF16), wp_ref[...], preferred_element_type=F32)
        gate = _sigmoid(jnp.dot(x.astype(BF16), wpg_ref[...], preferred_element_type=F32))
        x = _layer_norm(alpha * x + proj * gate, g2_ref[...], b2_ref[...])
    o_ref[...] = x


def _ffn_ln(x, wg, wu, wd, g, b, alpha, ple=None):
    m, d = x.shape
    dff = wd.shape[0]
    tm = TM_FFN
    row = lambda i: (i, 0)
    in_specs = [pl.BlockSpec((tm, d), row), _const_spec(wg.shape), _const_spec(wu.shape), _const_spec(wd.shape),
                _const_spec(g.shape), _const_spec(b.shape)]
    args = [x, wg, wu, wd, g, b]
    if ple is not None:
        p, wp, wpg, g2, b2 = ple
        in_specs += [pl.BlockSpec((tm, p.shape[1]), row), _const_spec(wp.shape), _const_spec(wpg.shape),
                     _const_spec(g2.shape), _const_spec(b2.shape)]
        args += [p, wp, wpg, g2, b2]
    return pl.pallas_call(
        functools.partial(_ffn_kernel, alpha=alpha, n_chunks=dff // FFN_CHUNK, with_ple=ple is not None),
        grid=(m // tm,),
        in_specs=in_specs,
        out_specs=pl.BlockSpec((tm, d), row),
        out_shape=jax.ShapeDtypeStruct((m, d), F32),
        scratch_shapes=[pltpu.VMEM((tm, dff), BF16)],
        compiler_params=pltpu.CompilerParams(dimension_semantics=("arbitrary",),
                                             vmem_limit_bytes=VMEM_LIMIT),
        name="ffn_ln_ple" if ple is not None else "ffn_ln",
    )(*args)


def _split3(x):
    hi = x.astype(BF16).astype(F32)
    r = x - hi
    mid = r.astype(BF16).astype(F32)
    lo = (r - mid).astype(BF16).astype(F32)
    return hi, mid, lo


def _mix_in_kernel(x_ref, w_ref, wf_ref, bf_ref, wbd_ref, pscale_ref, masks_ref,
                   qa_ref, ka_ref, va_ref, yp_ref, uext_ref, fcarry_ref, *, tm, n_heads, pool_w):
    si = pl.program_id(1)

    @pl.when(si == 0)
    def _():
        uext_ref[0:POOL_HALO, :] = jnp.zeros((POOL_HALO, pool_w), F32)
        fcarry_ref[...] = jnp.zeros_like(fcarry_ref)

    xb = x_ref[...].astype(BF16)
    z = jnp.dot(xb, w_ref[...], preferred_element_type=F32)

    u = z[:, :pool_w]
    uext_ref[POOL_HALO:POOL_HALO + tm, :] = u
    t_glob = si * tm + lax.broadcasted_iota(jnp.int32, (tm, 1), 0)
    pg = pool_w // len(POOL_WINDOWS)
    rs = []
    for gi, w in enumerate(POOL_WINDOWS):
        lanes = slice(gi * pg, (gi + 1) * pg)
        ug = u[:, lanes]
        acc = ug
        for j in range(1, w):
            acc = acc + uext_ref[POOL_HALO - j:POOL_HALO - j + tm, lanes]
        cnt = jnp.minimum(t_glob + 1, w).astype(F32)
        rs.append(acc / cnt - ug)
    r = jnp.concatenate(rs, axis=1).astype(BF16)
    y = jnp.dot(r, wbd_ref[...], preferred_element_type=F32) * pscale_ref[...]
    yp_ref[...] = y.astype(BF16)
    uext_ref[0:POOL_HALO, :] = uext_ref[tm:tm + POOL_HALO, :]

    fl = jnp.dot(xb, wf_ref[...], preferred_element_type=F32) + bf_ref[...]
    logf = jnp.minimum(fl, 0.0) - jnp.log1p(jnp.exp(-jnp.abs(fl)))
    tri = (lax.broadcasted_iota(jnp.int32, (tm, tm), 0)
           >= lax.broadcasted_iota(jnp.int32, (tm, tm), 1)).astype(BF16)
    a, b, c = _split3(logf)
    fcum = (jnp.dot(tri, a.astype(BF16), preferred_element_type=F32)
            + jnp.dot(tri, b.astype(BF16), preferred_element_type=F32)
            + jnp.dot(tri, c.astype(BF16), preferred_element_type=F32)) + fcarry_ref[...]
    fcarry_ref[...] = fcum[tm - 1:tm, :]

    hi, mid, lo = _split3(fcum * LOG2_E)
    mk = masks_ref[...]
    aq = hi * mk[0:1] + mid * mk[1:2] + lo * mk[2:3] + (mk[3:4] + mk[4:5] + mk[5:6])
    ak = (mk[0:1] + mk[1:2] + mk[2:3]) - (hi * mk[3:4] + mid * mk[4:5] + lo * mk[5:6])
    att_w = n_heads * HEAD_DIM
    scale = HEAD_DIM ** -0.5 * LOG2_E
    for h in range(n_heads):
        j, e = divmod(h, 2)
        dm = mk[6 + e:7 + e]
        am = mk[8 + h:9 + h]
        zq = z[:, pool_w + j * LANES:pool_w + (j + 1) * LANES]
        zk = z[:, pool_w + att_w + j * LANES:pool_w + att_w + (j + 1) * LANES]
        zv = z[:, pool_w + 2 * att_w + j * LANES:pool_w + 2 * att_w + (j + 1) * LANES]
        qa_ref[h] = (zq * (scale * dm) + aq * am).T.astype(BF16)
        ka_ref[h] = (zk * dm + ak * am).astype(BF16)
        va_ref[h] = (zv * dm + (1.0 - dm)).T.astype(BF16)


def _aug_lane_tables(n_heads):
    half = LANES // 2
    masks = np.zeros((8 + n_heads, LANES), np.float32)
    head_of_lane = np.full((LANES,), -1, np.int64)
    masks[6, :half] = 1.0
    masks[7, half:] = 1.0
    for h in range(n_heads):
        j, e = divmod(h, 2)
        base = half * (1 - e) + AUG_W * j
        for pos in range(AUG_W):
            masks[pos, base + pos] = 1.0
            masks[8 + h, base + pos] = 1.0
            head_of_lane[base + pos] = h
    return masks, head_of_lane


def _mix_in(x1, w_uqkv, wf_rep, bf_rep, wbd, pscale, masks, n_heads):
    bsz, seq, d = x1.shape
    tm = TM_MIX
    pool_w = wbd.shape[0]
    hspec = pl.BlockSpec((None, n_heads, tm, LANES), lambda b, s: (b, 0, s, 0))
    hshape = jax.ShapeDtypeStruct((bsz, n_heads, seq, LANES), BF16)
    tspec = pl.BlockSpec((None, n_heads, None, LANES, tm), lambda b, s: (b, 0, s, 0, 0))
    tshape = jax.ShapeDtypeStruct((bsz, n_heads, seq // tm, LANES, tm), BF16)
    return pl.pallas_call(
        functools.partial(_mix_in_kernel, tm=tm, n_heads=n_heads, pool_w=pool_w),
        grid=(bsz, seq // tm),
        in_specs=[pl.BlockSpec((None, tm, d), lambda b, s: (b, s, 0)),
                  _const_spec(w_uqkv.shape), _const_spec(wf_rep.shape), _const_spec(bf_rep.shape),
                  _const_spec(wbd.shape), _const_spec(pscale.shape), _const_spec(masks.shape)],
        out_specs=[tspec, hspec, tspec,
                   pl.BlockSpec((None, tm, pool_w), lambda b, s: (b, s, 0))],
        out_shape=[tshape, hshape, tshape, jax.ShapeDtypeStruct((bsz, seq, pool_w), BF16)],
        scratch_shapes=[pltpu.VMEM((POOL_HALO + tm, pool_w), F32), pltpu.VMEM((1, LANES), F32)],
        compiler_params=pltpu.CompilerParams(dimension_semantics=("arbitrary", "arbitrary"),
                                             vmem_limit_bytes=VMEM_LIMIT),
        name="mix_in",
    )(x1, w_uqkv, wf_rep, bf_rep, wbd, pscale, masks)


def _attn_kernel(qt_ref, ka_ref, vt_ref, o_ref, m_ref, acc_ref, *, tq, hps):
    qi = pl.program_id(2)

    def scores_t(e, kb):
        k0 = pl.multiple_of(kb * tq, tq)
        return jnp.dot(ka_ref[e, pl.ds(k0, tq), :], qt_ref[e], preferred_element_type=F32)

    causal = (lax.broadcasted_iota(jnp.int32, (tq, tq), 0)
              <= lax.broadcasted_iota(jnp.int32, (tq, tq), 1))

    def block(kb, diagonal):
        ss, ps = {}, {}
        for step in range(hps + QK_LOOKAHEAD + PV_LAG):
            if step < hps:
                ss[step] = scores_t(step, kb)
            e = step - QK_LOOKAHEAD
            if 0 <= e < hps:
                s = ss.pop(e)
                if diagonal:
                    s = jnp.where(causal, s, -jnp.inf)
                    m_new = jnp.max(s, axis=0, keepdims=True)
                    rescale = None
                else:
                    m_old = m_ref[e]
                    m_new = jnp.maximum(m_old, jnp.max(s, axis=0, keepdims=True))
                    rescale = jnp.exp2(m_old - m_new)
                m_ref[e] = m_new
                ps[e] = (jnp.exp2(s - m_new).astype(BF16), rescale)
            e = step - QK_LOOKAHEAD - PV_LAG
            if 0 <= e < hps:
                p_t, rescale = ps.pop(e)
                pv = jnp.dot(vt_ref[e, kb], p_t, preferred_element_type=F32)
                acc_ref[e] = pv if diagonal else rescale * acc_ref[e] + pv

    block(qi, diagonal=True)

    def body(kb, carry):
        block(kb, diagonal=False)
        return carry

    lax.fori_loop(0, qi, body, 0)

    outs = []
    for e in range(hps):
        acc = acc_ref[e]
        if e % 2 == 0:
            outs.append(acc[:HEAD_DIM] / acc[HEAD_DIM:HEAD_DIM + 1])
        else:
            outs.append(acc[HEAD_DIM:] / acc[0:1])
    o_ref[...] = jnp.concatenate(outs, axis=0).T.astype(BF16)


def _attention(qt, ka, vt):
    bsz, n_heads, n_blk, _, tq = qt.shape
    seq = n_blk * tq
    hps = HEADS_PER_STEP
    return pl.pallas_call(
        functools.partial(_attn_kernel, tq=tq, hps=hps),
        grid=(bsz, n_heads // hps, n_blk),
        in_specs=[pl.BlockSpec((None, hps, None, LANES, tq), lambda b, j, i: (b, j, i, 0, 0)),
                  pl.BlockSpec((None, hps, seq, LANES), lambda b, j, i: (b, j, 0, 0)),
                  pl.BlockSpec((None, hps, n_blk, LANES, tq), lambda b, j, i: (b, j, 0, 0, 0))],
        out_specs=pl.BlockSpec((None, tq, hps * HEAD_DIM), lambda b, j, i: (b, i, j)),
        out_shape=jax.ShapeDtypeStruct((bsz, seq, n_heads * HEAD_DIM), BF16),
        scratch_shapes=[pltpu.VMEM((hps, 1, tq), F32), pltpu.VMEM((hps, LANES, tq), F32)],
        compiler_params=pltpu.CompilerParams(dimension_semantics=("arbitrary",) * 3,
                                             vmem_limit_bytes=VMEM_LIMIT),
        name="fox_attention",
    )(qt, ka, vt)


def _mix_out_kernel(x_ref, o_ref, yp_ref, wgl_ref, bg_ref, wpu_ref, wau_ref, wo_ref, g_ref, b_ref,
                    out_ref, *, alpha):
    x = x_ref[...]
    d = x.shape[1]
    gates = _sigmoid(jnp.dot(x.astype(BF16), wgl_ref[...], preferred_element_type=F32) + bg_ref[...])
    y_pool = jnp.dot(yp_ref[...], wpu_ref[...], preferred_element_type=F32)
    y_att = jnp.dot(o_ref[...], wau_ref[...], preferred_element_type=F32)
    merged = gates[:, :d] * y_pool + gates[:, d:] * y_att
    y = jnp.dot(merged.astype(BF16), wo_ref[...], preferred_element_type=F32)
    out_ref[...] = _layer_norm(alpha * x + y, g_ref[...], b_ref[...])


def _mix_out(x1, o, yp, wgl, bg, wpu, wau, wo, g, b, alpha):
    m, d = x1.shape
    tm = TM_FFN
    row = lambda i: (i, 0)
    consts = [wgl, bg, wpu, wau, wo, g, b]
    return pl.pallas_call(
        functools.partial(_mix_out_kernel, alpha=alpha),
        grid=(m // tm,),
        in_specs=[pl.BlockSpec((tm, d), row), pl.BlockSpec((tm, o.shape[1]), row),
                  pl.BlockSpec((tm, yp.shape[1]), row)] + [_const_spec(c.shape) for c in consts],
        out_specs=pl.BlockSpec((tm, d), row),
        out_shape=jax.ShapeDtypeStruct((m, d), F32),
        compiler_params=pltpu.CompilerParams(dimension_semantics=("arbitrary",),
                                             vmem_limit_bytes=VMEM_LIMIT),
        name="mix_out",
    )(x1, o, yp, *consts)


def _block_diag(w_group):
    n, c, _ = w_group.shape
    out = jnp.zeros((n * c, n * c), w_group.dtype)
    for gi in range(n):
        out = out.at[gi * c:(gi + 1) * c, gi * c:(gi + 1) * c].set(w_group[gi])
    return out


def kernel(x, p, ffn1_w_gate, ffn1_w_up, ffn1_w_down, ln1_g, ln1_b, mix_w_in, mix_b_f, mix_b_gate, pool_w_group, pool_scale, pool_w_up, att_w_up, mix_w_out, ln2_g, ln2_b, ffn2_w_gate, ffn2_w_up, ffn2_w_down, ln3_g, ln3_b, ple_w_proj, ple_w_gate, ln4_g, ln4_b):
    bsz, seq, d = x.shape
    depth = p.shape[0]
    n_heads = mix_b_f.shape[1]
    pool_w = pool_scale.shape[1]
    att_w = n_heads * HEAD_DIM
    alpha = (2 * depth) ** 0.25
    masks_np, head_of_lane = _aug_lane_tables(n_heads)
    masks = jnp.asarray(masks_np)
    lane_sel = jnp.asarray(np.maximum(head_of_lane, 0))
    lane_on = jnp.asarray((head_of_lane >= 0).astype(np.float32))
    row2 = lambda v: v.reshape(1, -1)

    xf = x.reshape(bsz * seq, d)
    for i in range(depth):
        xf = _ffn_ln(xf, ffn1_w_gate[i].astype(BF16), ffn1_w_up[i].astype(BF16), ffn1_w_down[i].astype(BF16),
                     row2(ln1_g[i]), row2(ln1_b[i]), alpha)

        w_in = mix_w_in[i]
        qkv_end = pool_w + 3 * att_w
        w_f = w_in[:, qkv_end:qkv_end + n_heads]
        wf_rep = (w_f[:, lane_sel] * lane_on).astype(BF16)
        bf_rep = row2(mix_b_f[i][lane_sel] * lane_on)
        qa, ka, va, yp = _mix_in(xf.reshape(bsz, seq, d), w_in[:, :qkv_end].astype(BF16), wf_rep, bf_rep,
                                 _block_diag(pool_w_group[i]).astype(BF16), row2(pool_scale[i]), masks, n_heads)
        o = _attention(qa, ka, va)
        xf = _mix_out(xf, o.reshape(bsz * seq, att_w), yp.reshape(bsz * seq, pool_w),
                      w_in[:, qkv_end + n_heads:].astype(BF16), row2(mix_b_gate[i]),
                      pool_w_up[i].astype(BF16), att_w_up[i].astype(BF16), mix_w_out[i].astype(BF16),
                      row2(ln2_g[i]), row2(ln2_b[i]), alpha)

        xf = _ffn_ln(xf, ffn2_w_gate[i].astype(BF16), ffn2_w_up[i].astype(BF16), ffn2_w_down[i].astype(BF16),
                     row2(ln3_g[i]), row2(ln3_b[i]), alpha,
                     ple=(p[i].reshape(bsz * seq, -1), ple_w_proj[i].astype(BF16), ple_w_gate[i].astype(BF16),
                          row2(ln4_g[i]), row2(ln4_b[i])))
    return xf.reshape(bsz, seq, d)
```

```python
import functools

import jax
import jax.numpy as jnp
import numpy as np
from jax import lax
from jax.experimental import pallas as pl
from jax.experimental.pallas import tpu as pltpu

F32 = jnp.float32
BF16 = jnp.bfloat16

LN_EPS = 1e-5
POOL_WINDOWS = (2, 4, 8, 16)
HEAD_DIM = 64
LANES = 128
POOL_HALO = 16
AUG_W = 6
V7X_VMEM_BYTES = 64 * 1024 * 1024
VMEM_LIMIT = V7X_VMEM_BYTES * 7 // 8

FFN_CHUNK = 256
TM_FFN = 1024
FFN_SUB = 512
TM_MIX = 512
CUMSUM_BLOCK = 256
LOG2_E = float(np.log2(np.e))
QK_LOOKAHEAD = 2
PV_LAG = 0
HEADS_PER_STEP = 8


def _layer_norm(y, g, b):
    mu = jnp.mean(y, axis=-1, keepdims=True)
    yc = y - mu
    var = jnp.mean(yc * yc, axis=-1, keepdims=True)
    return yc * lax.rsqrt(var + LN_EPS) * g + b


def _sigmoid(x):
    return 1.0 / (1.0 + jnp.exp(-x))


def _const_spec(shape):
    nd = len(shape)
    return pl.BlockSpec(shape, lambda *_: (0,) * nd, pipeline_mode=pl.Buffered(1))


def _ffn_kernel(*refs, alpha, n_chunks, with_ple):
    if with_ple:
        (x_ref, wg_ref, wu_ref, wd_ref, g_ref, b_ref, p_ref, wp_ref, wpg_ref, g2_ref, b2_ref,
         o_ref, h_ref) = refs
    else:
        x_ref, wg_ref, wu_ref, wd_ref, g_ref, b_ref, o_ref, h_ref = refs
    fc = FFN_CHUNK
    sub = h_ref.shape[1]

    def finish(rows, y):
        x = _layer_norm(alpha * x_ref[rows, :] + 0.5 * y, g_ref[...], b_ref[...])
        if with_ple:
            proj = jnp.dot(p_ref[rows, :].astype(BF16), wp_ref[...], preferred_element_type=F32)
            gate = _sigmoid(jnp.dot(x.astype(BF16), wpg_ref[...], preferred_element_type=F32))
            x = _layer_norm(alpha * x + proj * gate, g2_ref[...], b2_ref[...])
        o_ref[rows, :] = x

    pending = None
    for s in range(x_ref.shape[0] // sub):
        rows = slice(s * sub, (s + 1) * sub)
        xb = x_ref[rows, :].astype(BF16)
        hbuf = h_ref.at[s % 2]
        for c in range(n_chunks):
            cols = slice(c * fc, (c + 1) * fc)
            g = jnp.dot(xb, wg_ref[:, cols], preferred_element_type=F32)
            u = jnp.dot(xb, wu_ref[:, cols], preferred_element_type=F32)
            hbuf[:, cols] = (g * _sigmoid(g) * u).astype(BF16)
            if c == 0 and pending is not None:
                finish(*pending)
        pending = (rows, jnp.dot(hbuf[...], wd_ref[...], preferred_element_type=F32))
    finish(*pending)


def _ffn_ln(x, wg, wu, wd, g, b, alpha, ple=None):
    m, d = x.shape
    dff = wd.shape[0]
    tm = TM_FFN
    row = lambda i: (i, 0)
    in_specs = [pl.BlockSpec((tm, d), row), _const_spec(wg.shape), _const_spec(wu.shape), _const_spec(wd.shape),
                _const_spec(g.shape), _const_spec(b.shape)]
    args = [x, wg, wu, wd, g, b]
    if ple is not None:
        p, wp, wpg, g2, b2 = ple
        in_specs += [pl.BlockSpec((tm, p.shape[1]), row), _const_spec(wp.shape), _const_spec(wpg.shape),
                     _const_spec(g2.shape), _const_spec(b2.shape)]
        args += [p, wp, wpg, g2, b2]
    return pl.pallas_call(
        functools.partial(_ffn_kernel, alpha=alpha, n_chunks=dff // FFN_CHUNK, with_ple=ple is not None),
        grid=(m // tm,),
        in_specs=in_specs,
        out_specs=pl.BlockSpec((tm, d), row),
        out_shape=jax.ShapeDtypeStruct((m, d), F32),
        scratch_shapes=[pltpu.VMEM((2, FFN_SUB, dff), BF16)],
        compiler_params=pltpu.CompilerParams(dimension_semantics=("arbitrary",),
                                             vmem_limit_bytes=VMEM_LIMIT),
        name="ffn_ln_ple" if ple is not None else "ffn_ln",
    )(*args)


def _split3(x):
    hi = x.astype(BF16).astype(F32)
    r = x - hi
    mid = r.astype(BF16).astype(F32)
    lo = (r - mid).astype(BF16).astype(F32)
    return hi, mid, lo


def _mix_in_kernel(x_ref, w_ref, wf_ref, bf_ref, wbd_ref, pscale_ref, masks_ref,
                   qa_ref, ka_ref, va_ref, yp_ref, uext_ref, fcarry_ref, *, tm, n_heads, pool_w):
    si = pl.program_id(1)

    @pl.when(si == 0)
    def _():
        uext_ref[0:POOL_HALO, :] = jnp.zeros((POOL_HALO, pool_w), F32)
        fcarry_ref[...] = jnp.zeros_like(fcarry_ref)

    xb = x_ref[...].astype(BF16)
    mk = masks_ref[...]

    att_w = n_heads * HEAD_DIM
    sec_w = 2 * LANES
    sections = [(kind, c0) for kind in (2, 0, 1) for c0 in range(0, att_w, sec_w)]

    def section_dot(kind, c0):
        col = pool_w + kind * att_w + c0
        return jnp.dot(xb, w_ref[:, col:col + sec_w], preferred_element_type=F32)

    fl = jnp.dot(xb, wf_ref[...], preferred_element_type=F32) + bf_ref[...]
    u = jnp.dot(xb, w_ref[:, :pool_w], preferred_element_type=F32)
    pending = section_dot(*sections[0])
    logf = jnp.minimum(fl, 0.0) - jnp.log1p(jnp.exp(-jnp.abs(fl)))
    cb = CUMSUM_BLOCK
    tri = (lax.broadcasted_iota(jnp.int32, (cb, cb), 0)
           >= lax.broadcasted_iota(jnp.int32, (cb, cb), 1)).astype(BF16)
    terms = [t.astype(BF16) for t in _split3(logf)]
    running = fcarry_ref[...]
    blocks = []
    for r0 in range(0, tm, cb):
        blk = running
        for t in terms:
            blk = blk + jnp.dot(tri, t[r0:r0 + cb], preferred_element_type=F32)
        blocks.append(blk)
        running = blk[cb - 1:cb, :]
    fcum = jnp.concatenate(blocks, axis=0)
    fcarry_ref[...] = running

    hi, mid, lo = _split3(fcum * LOG2_E)
    aq = hi * mk[0:1] + mid * mk[1:2] + lo * mk[2:3] + (mk[3:4] + mk[4:5] + mk[5:6])
    ak = (mk[0:1] + mk[1:2] + mk[2:3]) - (hi * mk[3:4] + mid * mk[4:5] + lo * mk[5:6])
    lo_half = lax.broadcasted_iota(jnp.int32, (tm, LANES), 1) < LANES // 2
    data_lanes = (lo_half, jnp.logical_not(lo_half))

    uext_ref[POOL_HALO:POOL_HALO + tm, :] = u
    t_glob = si * tm + lax.broadcasted_iota(jnp.int32, (tm, 1), 0)
    pg = pool_w // len(POOL_WINDOWS)
    rs = []
    for gi, w in enumerate(POOL_WINDOWS):
        lanes = slice(gi * pg, (gi + 1) * pg)
        wsum = uext_ref[:, lanes]
        span = 1
        while span < w:
            wsum = wsum + pltpu.roll(wsum, span, 0)
            span *= 2
        inv_cnt = 1.0 / jnp.minimum(t_glob + 1, w).astype(F32)
        rs.append(wsum[POOL_HALO:] * inv_cnt - u[:, lanes])
    r = jnp.concatenate(rs, axis=1).astype(BF16)
    y = jnp.dot(r, wbd_ref[...], preferred_element_type=F32) * pscale_ref[...]
    yp_ref[...] = y.astype(BF16)
    uext_ref[0:POOL_HALO, :] = uext_ref[tm:tm + POOL_HALO, :]

    scale = HEAD_DIM ** -0.5 * LOG2_E
    for idx, (kind, c0) in enumerate(sections):
        zsec = pending
        if idx + 1 < len(sections):
            pending = section_dot(*sections[idx + 1])
        for jj in range(sec_w // LANES):
            zp = zsec[:, jj * LANES:(jj + 1) * LANES]
            for e in range(2):
                h = 2 * (c0 // LANES + jj) + e
                if kind == 0:
                    qa_ref[h] = jnp.where(data_lanes[e], zp * scale, aq).T.astype(BF16)
                elif kind == 1:
                    ka_ref[h] = jnp.where(data_lanes[e], zp, ak * mk[8 + h:9 + h]).astype(BF16)
                else:
                    va_ref[h] = jnp.where(data_lanes[e], zp, 1.0).T.astype(BF16)


def _aug_lane_tables(n_heads):
    half = LANES // 2
    masks = np.zeros((8 + n_heads, LANES), np.float32)
    head_of_lane = np.full((LANES,), -1, np.int64)
    masks[6, :half] = 1.0
    masks[7, half:] = 1.0
    for h in range(n_heads):
        j, e = divmod(h, 2)
        base = half * (1 - e) + AUG_W * j
        for pos in range(AUG_W):
            masks[pos, base + pos] = 1.0
            masks[8 + h, base + pos] = 1.0
            head_of_lane[base + pos] = h
    return masks, head_of_lane


def _mix_in(x1, w_uqkv, wf_rep, bf_rep, wbd, pscale, masks, n_heads):
    bsz, seq, d = x1.shape
    tm = TM_MIX
    pool_w = wbd.shape[0]
    hspec = pl.BlockSpec((None, n_heads, tm, LANES), lambda b, s: (b, 0, s, 0))
    hshape = jax.ShapeDtypeStruct((bsz, n_heads, seq, LANES), BF16)
    tspec = pl.BlockSpec((None, n_heads, None, LANES, tm), lambda b, s: (b, 0, s, 0, 0))
    tshape = jax.ShapeDtypeStruct((bsz, n_heads, seq // tm, LANES, tm), BF16)
    return pl.pallas_call(
        functools.partial(_mix_in_kernel, tm=tm, n_heads=n_heads, pool_w=pool_w),
        grid=(bsz, seq // tm),
        in_specs=[pl.BlockSpec((None, tm, d), lambda b, s: (b, s, 0)),
                  _const_spec(w_uqkv.shape), _const_spec(wf_rep.shape), _const_spec(bf_rep.shape),
                  _const_spec(wbd.shape), _const_spec(pscale.shape), _const_spec(masks.shape)],
        out_specs=[tspec, hspec, tspec,
                   pl.BlockSpec((None, tm, pool_w), lambda b, s: (b, s, 0))],
        out_shape=[tshape, hshape, tshape, jax.ShapeDtypeStruct((bsz, seq, pool_w), BF16)],
        scratch_shapes=[pltpu.VMEM((POOL_HALO + tm, pool_w), F32), pltpu.VMEM((1, LANES), F32)],
        compiler_params=pltpu.CompilerParams(dimension_semantics=("arbitrary", "arbitrary"),
                                             vmem_limit_bytes=VMEM_LIMIT),
        name="mix_in",
    )(x1, w_uqkv, wf_rep, bf_rep, wbd, pscale, masks)


def _attn_kernel(qt_ref, ka_ref, vt_ref, o_ref, m_ref, acc_ref, *, tq, hps):
    qi = pl.program_id(2)

    def scores_t(e, kb):
        k0 = pl.multiple_of(kb * tq, tq)
        return jnp.dot(ka_ref[e, pl.ds(k0, tq), :], qt_ref[e], preferred_element_type=F32)

    causal = (lax.broadcasted_iota(jnp.int32, (tq, tq), 0)
              <= lax.broadcasted_iota(jnp.int32, (tq, tq), 1))

    def block(kb, diagonal):
        ss, ps = {}, {}
        for step in range(hps + QK_LOOKAHEAD + PV_LAG):
            if step < hps:
                ss[step] = scores_t(step, kb)
            e = step - QK_LOOKAHEAD
            if 0 <= e < hps:
                s = ss.pop(e)
                if diagonal:
                    s = jnp.where(causal, s, -jnp.inf)
                    m_new = jnp.max(s, axis=0, keepdims=True)
                    rescale = None
                else:
                    m_old = m_ref[e]
                    m_new = jnp.maximum(m_old, jnp.max(s, axis=0, keepdims=True))
                    rescale = jnp.exp2(m_old - m_new)
                m_ref[e] = m_new
                ps[e] = (jnp.exp2(s - m_new).astype(BF16), rescale)
            e = step - QK_LOOKAHEAD - PV_LAG
            if 0 <= e < hps:
                p_t, rescale = ps.pop(e)
                pv = jnp.dot(vt_ref[e, kb], p_t, preferred_element_type=F32)
                acc_ref[e] = pv if diagonal else rescale * acc_ref[e] + pv

    block(qi, diagonal=True)

    def body(kb, carry):
        block(kb, diagonal=False)
        return carry

    lax.fori_loop(0, qi, body, 0)

    outs = []
    for e in range(hps):
        acc = acc_ref[e]
        if e % 2 == 0:
            outs.append(acc[:HEAD_DIM] / acc[HEAD_DIM:HEAD_DIM + 1])
        else:
            outs.append(acc[HEAD_DIM:] / acc[0:1])
    o_ref[...] = jnp.concatenate(outs, axis=0).T.astype(BF16)


def _attention(qt, ka, vt):
    bsz, n_heads, n_blk, _, tq = qt.shape
    seq = n_blk * tq
    hps = HEADS_PER_STEP
    return pl.pallas_call(
        functools.partial(_attn_kernel, tq=tq, hps=hps),
        grid=(bsz, n_heads // hps, n_blk),
        in_specs=[pl.BlockSpec((None, hps, None, LANES, tq), lambda b, j, i: (b, j, i, 0, 0)),
                  pl.BlockSpec((None, hps, seq, LANES), lambda b, j, i: (b, j, 0, 0)),
                  pl.BlockSpec((None, hps, n_blk, LANES, tq), lambda b, j, i: (b, j, 0, 0, 0))],
        out_specs=pl.BlockSpec((None, tq, hps * HEAD_DIM), lambda b, j, i: (b, i, j)),
        out_shape=jax.ShapeDtypeStruct((bsz, seq, n_heads * HEAD_DIM), BF16),
        scratch_shapes=[pltpu.VMEM((hps, 1, tq), F32), pltpu.VMEM((hps, LANES, tq), F32)],
        compiler_params=pltpu.CompilerParams(dimension_semantics=("arbitrary",) * 3,
                                             vmem_limit_bytes=VMEM_LIMIT),
        name="fox_attention",
    )(qt, ka, vt)


def _mix_out_kernel(x_ref, o_ref, yp_ref, wgl_ref, bg_ref, wpu_ref, wau_ref, wo_ref, g_ref, b_ref,
                    out_ref, *, alpha):
    d = x_ref.shape[1]
    sub = FFN_SUB

    def branch_matmuls(rows):
        glog = jnp.dot(x_ref[rows, :].astype(BF16), wgl_ref[...], preferred_element_type=F32)
        y_pool = jnp.dot(yp_ref[rows, :], wpu_ref[...], preferred_element_type=F32)
        y_att = jnp.dot(o_ref[rows, :], wau_ref[...], preferred_element_type=F32)
        return rows, glog, y_pool, y_att

    def finish(rows, glog, y_pool, y_att):
        gates = _sigmoid(glog + bg_ref[...])
        merged = gates[:, :d] * y_pool + gates[:, d:] * y_att
        y = jnp.dot(merged.astype(BF16), wo_ref[...], preferred_element_type=F32)
        out_ref[rows, :] = _layer_norm(alpha * x_ref[rows, :] + y, g_ref[...], b_ref[...])

    n_sub = x_ref.shape[0] // sub
    pending = branch_matmuls(slice(0, sub))
    for s in range(n_sub):
        current = pending
        if s + 1 < n_sub:
            pending = branch_matmuls(slice((s + 1) * sub, (s + 2) * sub))
        finish(*current)


def _mix_out(x1, o, yp, wgl, bg, wpu, wau, wo, g, b, alpha):
    m, d = x1.shape
    tm = TM_FFN
    row = lambda i: (i, 0)
    consts = [wgl, bg, wpu, wau, wo, g, b]
    return pl.pallas_call(
        functools.partial(_mix_out_kernel, alpha=alpha),
        grid=(m // tm,),
        in_specs=[pl.BlockSpec((tm, d), row), pl.BlockSpec((tm, o.shape[1]), row),
                  pl.BlockSpec((tm, yp.shape[1]), row)] + [_const_spec(c.shape) for c in consts],
        out_specs=pl.BlockSpec((tm, d), row),
        out_shape=jax.ShapeDtypeStruct((m, d), F32),
        compiler_params=pltpu.CompilerParams(dimension_semantics=("arbitrary",),
                                             vmem_limit_bytes=VMEM_LIMIT),
        name="mix_out",
    )(x1, o, yp, *consts)


def _block_diag(w_group):
    n, c, _ = w_group.shape
    out = jnp.zeros((n * c, n * c), w_group.dtype)
    for gi in range(n):
        out = out.at[gi * c:(gi + 1) * c, gi * c:(gi + 1) * c].set(w_group[gi])
    return out


def kernel(x, p, ffn1_w_gate, ffn1_w_up, ffn1_w_down, ln1_g, ln1_b, mix_w_in, mix_b_f, mix_b_gate, pool_w_group, pool_scale, pool_w_up, att_w_up, mix_w_out, ln2_g, ln2_b, ffn2_w_gate, ffn2_w_up, ffn2_w_down, ln3_g, ln3_b, ple_w_proj, ple_w_gate, ln4_g, ln4_b):
    bsz, seq, d = x.shape
    depth = p.shape[0]
    n_heads = mix_b_f.shape[1]
    pool_w = pool_scale.shape[1]
    att_w = n_heads * HEAD_DIM
    alpha = (2 * depth) ** 0.25
    masks_np, head_of_lane = _aug_lane_tables(n_heads)
    masks = jnp.asarray(masks_np)
    lane_sel = jnp.asarray(np.maximum(head_of_lane, 0))
    lane_on = jnp.asarray((head_of_lane >= 0).astype(np.float32))
    row2 = lambda v: v.reshape(1, -1)

    xf = x.reshape(bsz * seq, d)
    for i in range(depth):
        xf = _ffn_ln(xf, ffn1_w_gate[i].astype(BF16), ffn1_w_up[i].astype(BF16), ffn1_w_down[i].astype(BF16),
                     row2(ln1_g[i]), row2(ln1_b[i]), alpha)

        w_in = mix_w_in[i]
        qkv_end = pool_w + 3 * att_w
        w_f = w_in[:, qkv_end:qkv_end + n_heads]
        wf_rep = (w_f[:, lane_sel] * lane_on).astype(BF16)
        bf_rep = row2(mix_b_f[i][lane_sel] * lane_on)
        qa, ka, va, yp = _mix_in(xf.reshape(bsz, seq, d), w_in[:, :qkv_end].astype(BF16), wf_rep, bf_rep,
                                 _block_diag(pool_w_group[i]).astype(BF16), row2(pool_scale[i]), masks, n_heads)
        o = _attention(qa, ka, va)
        xf = _mix_out(xf, o.reshape(bsz * seq, att_w), yp.reshape(bsz * seq, pool_w),
                      w_in[:, qkv_end + n_heads:].astype(BF16), row2(mix_b_gate[i]),
                      pool_w_up[i].astype(BF16), att_w_up[i].astype(BF16), mix_w_out[i].astype(BF16),
                      row2(ln2_g[i]), row2(ln2_b[i]), alpha)

        xf = _ffn_ln(xf, ffn2_w_gate[i].astype(BF16), ffn2_w_up[i].astype(BF16), ffn2_w_down[i].astype(BF16),
                     row2(ln3_g[i]), row2(ln3_b[i]), alpha,
                     ple=(p[i].reshape(bsz * seq, -1), ple_w_proj[i].astype(BF16), ple_w_gate[i].astype(BF16),
                          row2(ln4_g[i]), row2(ln4_b[i])))
    return xf.reshape(bsz, seq, d)
```

```python
import functools

import jax
import jax.numpy as jnp
import numpy as np
from jax import lax
from jax.experimental import pallas as pl
from jax.experimental.pallas import tpu as pltpu

F32 = jnp.float32
BF16 = jnp.bfloat16

LN_EPS = 1e-5
POOL_WINDOWS = (2, 4, 8, 16)
HEAD_DIM = 64
LANES = 128
POOL_HALO = 16
AUG_W = 6
V7X_VMEM_BYTES = 64 * 1024 * 1024
VMEM_LIMIT = V7X_VMEM_BYTES * 7 // 8

FFN_CHUNK = 256
TM_FFN = 1024
TM_MIX_OUT = 512
FFN_SUB = 512
TM_MIX = 512
CUMSUM_BLOCK = 256
LOG2_E = float(np.log2(np.e))
HEADS_PER_STEP = 8
ATTN_STRIP = 256
KV_UNROLL = 2
QK_LOOKAHEAD = 4
PV_LAG = 0


def _layer_norm(y, g, b):
    mu = jnp.mean(y, axis=-1, keepdims=True)
    yc = y - mu
    var = jnp.mean(yc * yc, axis=-1, keepdims=True)
    return yc * lax.rsqrt(var + LN_EPS) * g + b


def _sigmoid(x):
    return 1.0 / (1.0 + jnp.exp(-x))


def _const_spec(shape):
    nd = len(shape)
    return pl.BlockSpec(shape, lambda *_: (0,) * nd, pipeline_mode=pl.Buffered(1))


def _ffn_kernel(*refs, alpha, n_chunks, with_ple):
    if with_ple:
        (x_ref, wg_ref, wu_ref, wd_ref, g_ref, b_ref, p_ref, wp_ref, wpg_ref, g2_ref, b2_ref,
         o_ref, h_ref) = refs
    else:
        x_ref, wg_ref, wu_ref, wd_ref, g_ref, b_ref, o_ref, h_ref = refs
    fc = FFN_CHUNK
    sub = h_ref.shape[1]

    def finish(rows, y):
        x = _layer_norm(alpha * x_ref[rows, :] + 0.5 * y, g_ref[...], b_ref[...])
        if with_ple:
            proj = jnp.dot(p_ref[rows, :].astype(BF16), wp_ref[...], preferred_element_type=F32)
            gate = _sigmoid(jnp.dot(x.astype(BF16), wpg_ref[...], preferred_element_type=F32))
            x = _layer_norm(alpha * x + proj * gate, g2_ref[...], b2_ref[...])
        o_ref[rows, :] = x

    pending = None
    for s in range(x_ref.shape[0] // sub):
        rows = slice(s * sub, (s + 1) * sub)
        xb = x_ref[rows, :].astype(BF16)
        hbuf = h_ref.at[s % 2]
        for c in range(n_chunks):
            cols = slice(c * fc, (c + 1) * fc)
            g = jnp.dot(xb, wg_ref[:, cols], preferred_element_type=F32)
            u = jnp.dot(xb, wu_ref[:, cols], preferred_element_type=F32)
            hbuf[:, cols] = (g * _sigmoid(g) * u).astype(BF16)
            if c == 0 and pending is not None:
                finish(*pending)
        pending = (rows, jnp.dot(hbuf[...], wd_ref[...], preferred_element_type=F32))
    finish(*pending)


def _ffn_ln(x, wg, wu, wd, g, b, alpha, ple=None):
    m, d = x.shape
    dff = wd.shape[0]
    tm = TM_FFN
    row = lambda i: (i, 0)
    in_specs = [pl.BlockSpec((tm, d), row), _const_spec(wg.shape), _const_spec(wu.shape), _const_spec(wd.shape),
                _const_spec(g.shape), _const_spec(b.shape)]
    args = [x, wg, wu, wd, g, b]
    if ple is not None:
        p, wp, wpg, g2, b2 = ple
        in_specs += [pl.BlockSpec((tm, p.shape[1]), row), _const_spec(wp.shape), _const_spec(wpg.shape),
                     _const_spec(g2.shape), _const_spec(b2.shape)]
        args += [p, wp, wpg, g2, b2]
    return pl.pallas_call(
        functools.partial(_ffn_kernel, alpha=alpha, n_chunks=dff // FFN_CHUNK, with_ple=ple is not None),
        grid=(m // tm,),
        in_specs=in_specs,
        out_specs=pl.BlockSpec((tm, d), row),
        out_shape=jax.ShapeDtypeStruct((m, d), F32),
        scratch_shapes=[pltpu.VMEM((2, FFN_SUB, dff), BF16)],
        compiler_params=pltpu.CompilerParams(dimension_semantics=("arbitrary",),
                                             vmem_limit_bytes=VMEM_LIMIT),
        name="ffn_ln_ple" if ple is not None else "ffn_ln",
    )(*args)


def _split3(x):
    hi = x.astype(BF16).astype(F32)
    r = x - hi
    mid = r.astype(BF16).astype(F32)
    lo = (r - mid).astype(BF16).astype(F32)
    return hi, mid, lo


def _mix_in_kernel(x_ref, w_ref, wf_ref, bf_ref, wbd_ref, pscale_ref, masks_ref,
                   qa_ref, ka_ref, va_ref, yp_ref, uext_ref, fcarry_ref, *, tm, n_heads, pool_w):
    si = pl.program_id(1)

    @pl.when(si == 0)
    def _():
        uext_ref[0:POOL_HALO, :] = jnp.zeros((POOL_HALO, pool_w), F32)
        fcarry_ref[...] = jnp.zeros_like(fcarry_ref)

    xb = x_ref[...].astype(BF16)
    mk = masks_ref[...]

    att_w = n_heads * HEAD_DIM
    sec_w = 2 * LANES
    sections = [(kind, c0) for kind in (2, 0, 1) for c0 in range(0, att_w, sec_w)]

    def section_dot(kind, c0):
        col = pool_w + kind * att_w + c0
        return jnp.dot(xb, w_ref[:, col:col + sec_w], preferred_element_type=F32)

    fl = jnp.dot(xb, wf_ref[...], preferred_element_type=F32) + bf_ref[...]
    u = jnp.dot(xb, w_ref[:, :pool_w], preferred_element_type=F32)
    pending = section_dot(*sections[0])
    logf = jnp.minimum(fl, 0.0) - jnp.log1p(jnp.exp(-jnp.abs(fl)))
    cb = CUMSUM_BLOCK
    tri = (lax.broadcasted_iota(jnp.int32, (cb, cb), 0)
           >= lax.broadcasted_iota(jnp.int32, (cb, cb), 1)).astype(BF16)
    terms = [t.astype(BF16) for t in _split3(logf)]
    running = fcarry_ref[...]
    blocks = []
    for r0 in range(0, tm, cb):
        blk = running
        for t in terms:
            blk = blk + jnp.dot(tri, t[r0:r0 + cb], preferred_element_type=F32)
        blocks.append(blk)
        running = blk[cb - 1:cb, :]
    fcum = jnp.concatenate(blocks, axis=0)
    fcarry_ref[...] = running

    hi, mid, lo = _split3(fcum * LOG2_E)
    aq = hi * mk[0:1] + mid * mk[1:2] + lo * mk[2:3] + (mk[3:4] + mk[4:5] + mk[5:6])
    ak = (mk[0:1] + mk[1:2] + mk[2:3]) - (hi * mk[3:4] + mid * mk[4:5] + lo * mk[5:6])
    lo_half = lax.broadcasted_iota(jnp.int32, (tm, LANES), 1) < LANES // 2
    data_lanes = (lo_half, jnp.logical_not(lo_half))

    uext_ref[POOL_HALO:POOL_HALO + tm, :] = u
    t_glob = si * tm + lax.broadcasted_iota(jnp.int32, (tm, 1), 0)
    pg = pool_w // len(POOL_WINDOWS)
    rs = []
    for gi, w in enumerate(POOL_WINDOWS):
        lanes = slice(gi * pg, (gi + 1) * pg)
        wsum = uext_ref[:, lanes]
        span = 1
        while span < w:
            wsum = wsum + pltpu.roll(wsum, span, 0)
            span *= 2
        inv_cnt = 1.0 / jnp.minimum(t_glob + 1, w).astype(F32)
        rs.append(wsum[POOL_HALO:] * inv_cnt - u[:, lanes])
    r = jnp.concatenate(rs, axis=1).astype(BF16)
    y = jnp.dot(r, wbd_ref[...], preferred_element_type=F32) * pscale_ref[...]
    yp_ref[...] = y.astype(BF16)
    uext_ref[0:POOL_HALO, :] = uext_ref[tm:tm + POOL_HALO, :]

    scale = HEAD_DIM ** -0.5 * LOG2_E
    for idx, (kind, c0) in enumerate(sections):
        zsec = pending
        if idx + 1 < len(sections):
            pending = section_dot(*sections[idx + 1])
        for jj in range(sec_w // LANES):
            zp = zsec[:, jj * LANES:(jj + 1) * LANES]
            for e in range(2):
                h = 2 * (c0 // LANES + jj) + e
                if kind == 0:
                    qa_ref[h] = jnp.where(data_lanes[e], zp * scale, aq).T.astype(BF16)
                elif kind == 1:
                    ka_ref[h] = jnp.where(data_lanes[e], zp, ak * mk[8 + h:9 + h]).astype(BF16)
                else:
                    va_ref[h] = jnp.where(data_lanes[e], zp, 1.0).T.astype(BF16)


def _aug_lane_tables(n_heads):
    half = LANES // 2
    masks = np.zeros((8 + n_heads, LANES), np.float32)
    head_of_lane = np.full((LANES,), -1, np.int64)
    masks[6, :half] = 1.0
    masks[7, half:] = 1.0
    for h in range(n_heads):
        j, e = divmod(h, 2)
        base = half * (1 - e) + AUG_W * j
        for pos in range(AUG_W):
            masks[pos, base + pos] = 1.0
            masks[8 + h, base + pos] = 1.0
            head_of_lane[base + pos] = h
    return masks, head_of_lane


def _mix_in(x1, w_uqkv, wf_rep, bf_rep, wbd, pscale, masks, n_heads):
    bsz, seq, d = x1.shape
    tm = TM_MIX
    pool_w = wbd.shape[0]
    hspec = pl.BlockSpec((None, n_heads, tm, LANES), lambda b, s: (b, 0, s, 0))
    hshape = jax.ShapeDtypeStruct((bsz, n_heads, seq, LANES), BF16)
    tspec = pl.BlockSpec((None, n_heads, None, LANES, tm), lambda b, s: (b, 0, s, 0, 0))
    tshape = jax.ShapeDtypeStruct((bsz, n_heads, seq // tm, LANES, tm), BF16)
    return pl.pallas_call(
        functools.partial(_mix_in_kernel, tm=tm, n_heads=n_heads, pool_w=pool_w),
        grid=(bsz, seq // tm),
        in_specs=[pl.BlockSpec((None, tm, d), lambda b, s: (b, s, 0)),
                  _const_spec(w_uqkv.shape), _const_spec(wf_rep.shape), _const_spec(bf_rep.shape),
                  _const_spec(wbd.shape), _const_spec(pscale.shape), _const_spec(masks.shape)],
        out_specs=[tspec, hspec, tspec,
                   pl.BlockSpec((None, tm, pool_w), lambda b, s: (b, s, 0))],
        out_shape=[tshape, hshape, tshape, jax.ShapeDtypeStruct((bsz, seq, pool_w), BF16)],
        scratch_shapes=[pltpu.VMEM((POOL_HALO + tm, pool_w), F32), pltpu.VMEM((1, LANES), F32)],
        compiler_params=pltpu.CompilerParams(dimension_semantics=("arbitrary", "arbitrary"),
                                             vmem_limit_bytes=VMEM_LIMIT),
        name="mix_in",
    )(x1, w_uqkv, wf_rep, bf_rep, wbd, pscale, masks)


def _attn_kernel(qt_ref, ka_ref, vt_ref, o_ref, m_ref, acc_ref, *, tq, hps):
    qi = pl.program_id(2)

    qw = ATTN_STRIP
    strips = [slice(c, c + qw) for c in range(0, tq, qw)]

    def scores_t(e, kb, cols):
        k0 = pl.multiple_of(kb * tq, tq)
        return jnp.dot(ka_ref[e, pl.ds(k0, tq), :], qt_ref[e, :, cols], preferred_element_type=F32)

    key_idx = lax.broadcasted_iota(jnp.int32, (tq, qw), 0)
    query_idx = lax.broadcasted_iota(jnp.int32, (tq, qw), 1)

    def blocks(kbs, diagonal=False):
        units = [(kb, e, cols) for kb in kbs for e in range(hps) for cols in strips]
        ss, ps = {}, {}
        for step in range(len(units) + QK_LOOKAHEAD + PV_LAG):
            if step < len(units):
                kb, e, cols = units[step]
                ss[step] = scores_t(e, kb, cols)
            i = step - QK_LOOKAHEAD
            if 0 <= i < len(units):
                kb, e, cols = units[i]
                s = ss.pop(i)
                if diagonal:
                    s = jnp.where(key_idx <= query_idx + cols.start, s, -jnp.inf)
                    m_new = jnp.max(s, axis=0, keepdims=True)
                    rescale = None
                else:
                    m_old = m_ref[e, :, cols]
                    m_new = jnp.maximum(m_old, jnp.max(s, axis=0, keepdims=True))
                    rescale = jnp.exp2(m_old - m_new)
                m_ref[e, :, cols] = m_new
                ps[i] = (jnp.exp2(s - m_new).astype(BF16), rescale)
            i = step - QK_LOOKAHEAD - PV_LAG
            if 0 <= i < len(units):
                kb, e, cols = units[i]
                p_t, rescale = ps.pop(i)
                pv = jnp.dot(vt_ref[e, kb], p_t, preferred_element_type=F32)
                acc_ref[e, :, cols] = pv if diagonal else rescale * acc_ref[e, :, cols] + pv

    blocks([qi], diagonal=True)

    def body(i, carry):
        blocks([i * KV_UNROLL + r for r in range(KV_UNROLL)])
        return carry

    n_full = qi // KV_UNROLL
    lax.fori_loop(0, n_full, body, 0)

    def tail(kb, carry):
        blocks([kb])
        return carry

    lax.fori_loop(n_full * KV_UNROLL, qi, tail, 0)

    outs = []
    for e in range(hps):
        acc = acc_ref[e]
        if e % 2 == 0:
            outs.append(acc[:HEAD_DIM] / acc[HEAD_DIM:HEAD_DIM + 1])
        else:
            outs.append(acc[HEAD_DIM:] / acc[0:1])
    o_ref[...] = jnp.concatenate(outs, axis=0).T.astype(BF16)


def _attention(qt, ka, vt):
    bsz, n_heads, n_blk, _, tq = qt.shape
    seq = n_blk * tq
    hps = HEADS_PER_STEP
    return pl.pallas_call(
        functools.partial(_attn_kernel, tq=tq, hps=hps),
        grid=(bsz, n_heads // hps, n_blk),
        in_specs=[pl.BlockSpec((None, hps, None, LANES, tq), lambda b, j, i: (b, j, i, 0, 0)),
                  pl.BlockSpec((None, hps, seq, LANES), lambda b, j, i: (b, j, 0, 0)),
                  pl.BlockSpec((None, hps, n_blk, LANES, tq), lambda b, j, i: (b, j, 0, 0, 0))],
        out_specs=pl.BlockSpec((None, tq, hps * HEAD_DIM), lambda b, j, i: (b, i, j)),
        out_shape=jax.ShapeDtypeStruct((bsz, seq, n_heads * HEAD_DIM), BF16),
        scratch_shapes=[pltpu.VMEM((hps, 1, tq), F32), pltpu.VMEM((hps, LANES, tq), F32)],
        compiler_params=pltpu.CompilerParams(dimension_semantics=("arbitrary",) * 3,
                                             vmem_limit_bytes=VMEM_LIMIT),
        name="fox_attention",
    )(qt, ka, vt)


def _mix_out_kernel(x_ref, o_ref, yp_ref, wgl_ref, bg_ref, wpu_ref, wau_ref, wo_ref, g_ref, b_ref,
                    out_ref, *, alpha):
    d = x_ref.shape[1]
    sub = FFN_SUB

    def branch_matmuls(rows):
        glog = jnp.dot(x_ref[rows, :].astype(BF16), wgl_ref[...], preferred_element_type=F32)
        y_pool = jnp.dot(yp_ref[rows, :], wpu_ref[...], preferred_element_type=F32)
        y_att = jnp.dot(o_ref[rows, :], wau_ref[...], preferred_element_type=F32)
        return rows, glog, y_pool, y_att

    def finish(rows, glog, y_pool, y_att):
        gates = _sigmoid(glog + bg_ref[...])
        merged = gates[:, :d] * y_pool + gates[:, d:] * y_att
        y = jnp.dot(merged.astype(BF16), wo_ref[...], preferred_element_type=F32)
        out_ref[rows, :] = _layer_norm(alpha * x_ref[rows, :] + y, g_ref[...], b_ref[...])

    n_sub = x_ref.shape[0] // sub
    pending = branch_matmuls(slice(0, sub))
    for s in range(n_sub):
        current = pending
        if s + 1 < n_sub:
            pending = branch_matmuls(slice((s + 1) * sub, (s + 2) * sub))
        finish(*current)


def _mix_out(x1, o, yp, wgl, bg, wpu, wau, wo, g, b, alpha):
    m, d = x1.shape
    tm = TM_MIX_OUT
    row = lambda i: (i, 0)
    consts = [wgl, bg, wpu, wau, wo, g, b]
    return pl.pallas_call(
        functools.partial(_mix_out_kernel, alpha=alpha),
        grid=(m // tm,),
        in_specs=[pl.BlockSpec((tm, d), row), pl.BlockSpec((tm, o.shape[1]), row),
                  pl.BlockSpec((tm, yp.shape[1]), row)] + [_const_spec(c.shape) for c in consts],
        out_specs=pl.BlockSpec((tm, d), row),
        out_shape=jax.ShapeDtypeStruct((m, d), F32),
        compiler_params=pltpu.CompilerParams(dimension_semantics=("arbitrary",),
                                             vmem_limit_bytes=VMEM_LIMIT),
        name="mix_out",
    )(x1, o, yp, *consts)


def _block_diag(w_group):
    n, c, _ = w_group.shape
    out = jnp.zeros((n * c, n * c), w_group.dtype)
    for gi in range(n):
        out = out.at[gi * c:(gi + 1) * c, gi * c:(gi + 1) * c].set(w_group[gi])
    return out


def kernel(x, p, ffn1_w_gate, ffn1_w_up, ffn1_w_down, ln1_g, ln1_b, mix_w_in, mix_b_f, mix_b_gate, pool_w_group, pool_scale, pool_w_up, att_w_up, mix_w_out, ln2_g, ln2_b, ffn2_w_gate, ffn2_w_up, ffn2_w_down, ln3_g, ln3_b, ple_w_proj, ple_w_gate, ln4_g, ln4_b):
    bsz, seq, d = x.shape
    depth = p.shape[0]
    n_heads = mix_b_f.shape[1]
    pool_w = pool_scale.shape[1]
    att_w = n_heads * HEAD_DIM
    alpha = (2 * depth) ** 0.25
    masks_np, head_of_lane = _aug_lane_tables(n_heads)
    masks = jnp.asarray(masks_np)
    lane_sel = jnp.asarray(np.maximum(head_of_lane, 0))
    lane_on = jnp.asarray((head_of_lane >= 0).astype(np.float32))
    row2 = lambda v: v.reshape(1, -1)

    xf = x.reshape(bsz * seq, d)
    for i in range(depth):
        xf = _ffn_ln(xf, ffn1_w_gate[i].astype(BF16), ffn1_w_up[i].astype(BF16), ffn1_w_down[i].astype(BF16),
                     row2(ln1_g[i]), row2(ln1_b[i]), alpha)

        w_in = mix_w_in[i]
        qkv_end = pool_w + 3 * att_w
        w_f = w_in[:, qkv_end:qkv_end + n_heads]
        wf_rep = (w_f[:, lane_sel] * lane_on).astype(BF16)
        bf_rep = row2(mix_b_f[i][lane_sel] * lane_on)
        qa, ka, va, yp = _mix_in(xf.reshape(bsz, seq, d), w_in[:, :qkv_end].astype(BF16), wf_rep, bf_rep,
                                 _block_diag(pool_w_group[i]).astype(BF16), row2(pool_scale[i]), masks, n_heads)
        o = _attention(qa, ka, va)
        xf = _mix_out(xf, o.reshape(bsz * seq, att_w), yp.reshape(bsz * seq, pool_w),
                      w_in[:, qkv_end + n_heads:].astype(BF16), row2(mix_b_gate[i]),
                      pool_w_up[i].astype(BF16), att_w_up[i].astype(BF16), mix_w_out[i].astype(BF16),
                      row2(ln2_g[i]), row2(ln2_b[i]), alpha)

        xf = _ffn_ln(xf, ffn2_w_gate[i].astype(BF16), ffn2_w_up[i].astype(BF16), ffn2_w_down[i].astype(BF16),
                     row2(ln3_g[i]), row2(ln3_b[i]), alpha,
                     ple=(p[i].reshape(bsz * seq, -1), ple_w_proj[i].astype(BF16), ple_w_gate[i].astype(BF16),
                          row2(ln4_g[i]), row2(ln4_b[i])))
    return xf.reshape(bsz, seq, d)
```

```python
import functools

import jax
import jax.numpy as jnp
import numpy as np
from jax import lax
from jax.experimental import pallas as pl
from jax.experimental.pallas import tpu as pltpu

F32 = jnp.float32
BF16 = jnp.bfloat16

LN_EPS = 1e-5
POOL_WINDOWS = (2, 4, 8, 16)
HEAD_DIM = 64
LANES = 128
POOL_HALO = 16
VT_ONES = 16
VT_ROWS = HEAD_DIM + VT_ONES
AUG_W = 6
V7X_VMEM_BYTES = 64 * 1024 * 1024
VMEM_LIMIT = V7X_VMEM_BYTES * 7 // 8

FFN_CHUNK = 256
TM_FFN = 1024
FFN_SUB = 512
TM_MIX_OUT = 512
TM_MIX = 512
CUMSUM_BLOCK = 256
LOG2_E = float(np.log2(np.e))
HEADS_PER_STEP = 8
ATTN_STRIP = 256
KV_UNROLL = 2
QK_LOOKAHEAD = 4
PV_LAG = 0


def _layer_norm(y, g, b):
    mu = jnp.mean(y, axis=-1, keepdims=True)
    yc = y - mu
    var = jnp.mean(yc * yc, axis=-1, keepdims=True)
    return yc * lax.rsqrt(var + LN_EPS) * g + b


def _sigmoid(x):
    return 1.0 / (1.0 + jnp.exp(-x))


def _const_spec(shape):
    nd = len(shape)
    return pl.BlockSpec(shape, lambda *_: (0,) * nd, pipeline_mode=pl.Buffered(1))


def _ffn_kernel(*refs, alpha, n_chunks, with_ple):
    if with_ple:
        (x_ref, wg_ref, wu_ref, wd_ref, g_ref, b_ref, p_ref, wp_ref, wpg_ref, g2_ref, b2_ref,
         o_ref, h_ref) = refs
    else:
        x_ref, wg_ref, wu_ref, wd_ref, g_ref, b_ref, o_ref, h_ref = refs
    fc = FFN_CHUNK
    sub = h_ref.shape[1]

    def finish(rows, y):
        x = _layer_norm(alpha * x_ref[rows, :] + 0.5 * y, g_ref[...], b_ref[...])
        if with_ple:
            proj = jnp.dot(p_ref[rows, :].astype(BF16), wp_ref[...], preferred_element_type=F32)
            gate = _sigmoid(jnp.dot(x.astype(BF16), wpg_ref[...], preferred_element_type=F32))
            x = _layer_norm(alpha * x + proj * gate, g2_ref[...], b2_ref[...])
        o_ref[rows, :] = x

    pending = None
    for s in range(x_ref.shape[0] // sub):
        rows = slice(s * sub, (s + 1) * sub)
        xb = x_ref[rows, :].astype(BF16)
        hbuf = h_ref.at[s % 2]
        for c in range(n_chunks):
            cols = slice(c * fc, (c + 1) * fc)
            g = jnp.dot(xb, wg_ref[:, cols], preferred_element_type=F32)
            u = jnp.dot(xb, wu_ref[:, cols], preferred_element_type=F32)
            hbuf[:, cols] = (g * _sigmoid(g) * u).astype(BF16)
            if c == 0 and pending is not None:
                finish(*pending)
        pending = (rows, jnp.dot(hbuf[...], wd_ref[...], preferred_element_type=F32))
    finish(*pending)


def _ffn_ln(x, wg, wu, wd, g, b, alpha, ple=None):
    m, d = x.shape
    dff = wd.shape[0]
    tm = TM_FFN
    row = lambda i: (i, 0)
    in_specs = [pl.BlockSpec((tm, d), row), _const_spec(wg.shape), _const_spec(wu.shape), _const_spec(wd.shape),
                _const_spec(g.shape), _const_spec(b.shape)]
    args = [x, wg, wu, wd, g, b]
    if ple is not None:
        p, wp, wpg, g2, b2 = ple
        in_specs += [pl.BlockSpec((tm, p.shape[1]), row), _const_spec(wp.shape), _const_spec(wpg.shape),
                     _const_spec(g2.shape), _const_spec(b2.shape)]
        args += [p, wp, wpg, g2, b2]
    return pl.pallas_call(
        functools.partial(_ffn_kernel, alpha=alpha, n_chunks=dff // FFN_CHUNK, with_ple=ple is not None),
        grid=(m // tm,),
        in_specs=in_specs,
        out_specs=pl.BlockSpec((tm, d), row),
        out_shape=jax.ShapeDtypeStruct((m, d), F32),
        scratch_shapes=[pltpu.VMEM((2, FFN_SUB, dff), BF16)],
        compiler_params=pltpu.CompilerParams(dimension_semantics=("arbitrary",),
                                             vmem_limit_bytes=VMEM_LIMIT),
        name="ffn_ln_ple" if ple is not None else "ffn_ln",
    )(*args)


def _split3(x):
    hi = x.astype(BF16).astype(F32)
    r = x - hi
    mid = r.astype(BF16).astype(F32)
    lo = (r - mid).astype(BF16).astype(F32)
    return hi, mid, lo


def _mix_in_kernel(x_ref, w_ref, wf_ref, bf_ref, wbd_ref, pscale_ref, masks_ref,
                   qa_ref, ka_ref, va_ref, yp_ref, uext_ref, fcarry_ref, *, tm, n_heads, pool_w):
    si = pl.program_id(1)

    @pl.when(si == 0)
    def _():
        uext_ref[0:POOL_HALO, :] = jnp.zeros((POOL_HALO, pool_w), F32)
        fcarry_ref[...] = jnp.zeros_like(fcarry_ref)

    xb = x_ref[...].astype(BF16)
    mk = masks_ref[...]

    att_w = n_heads * HEAD_DIM
    sec_w = 2 * LANES
    sections = [(kind, c0) for kind in (2, 0, 1) for c0 in range(0, att_w, sec_w)]

    def section_dot(kind, c0):
        col = pool_w + kind * att_w + c0
        return jnp.dot(xb, w_ref[:, col:col + sec_w], preferred_element_type=F32)

    fl = jnp.dot(xb, wf_ref[...], preferred_element_type=F32) + bf_ref[...]
    u = jnp.dot(xb, w_ref[:, :pool_w], preferred_element_type=F32)
    pending = section_dot(*sections[0])
    logf = jnp.minimum(fl, 0.0) - jnp.log1p(jnp.exp(-jnp.abs(fl)))
    cb = CUMSUM_BLOCK
    tri = (lax.broadcasted_iota(jnp.int32, (cb, cb), 0)
           >= lax.broadcasted_iota(jnp.int32, (cb, cb), 1)).astype(BF16)
    terms = [t.astype(BF16) for t in _split3(logf)]
    running = fcarry_ref[...]
    blocks = []
    for r0 in range(0, tm, cb):
        blk = running
        for t in terms:
            blk = blk + jnp.dot(tri, t[r0:r0 + cb], preferred_element_type=F32)
        blocks.append(blk)
        running = blk[cb - 1:cb, :]
    fcum = jnp.concatenate(blocks, axis=0)
    fcarry_ref[...] = running

    hi, mid, lo = _split3(fcum * LOG2_E)
    aq = hi * mk[0:1] + mid * mk[1:2] + lo * mk[2:3] + (mk[3:4] + mk[4:5] + mk[5:6])
    ak = (mk[0:1] + mk[1:2] + mk[2:3]) - (hi * mk[3:4] + mid * mk[4:5] + lo * mk[5:6])
    lo_half = lax.broadcasted_iota(jnp.int32, (tm, LANES), 1) < LANES // 2
    data_lanes = (lo_half, jnp.logical_not(lo_half))

    uext_ref[POOL_HALO:POOL_HALO + tm, :] = u
    t_glob = si * tm + lax.broadcasted_iota(jnp.int32, (tm, 1), 0)
    pg = pool_w // len(POOL_WINDOWS)
    rs = []
    for gi, w in enumerate(POOL_WINDOWS):
        lanes = slice(gi * pg, (gi + 1) * pg)
        wsum = uext_ref[:, lanes]
        span = 1
        while span < w:
            wsum = wsum + pltpu.roll(wsum, span, 0)
            span *= 2
        inv_cnt = 1.0 / jnp.minimum(t_glob + 1, w).astype(F32)
        rs.append(wsum[POOL_HALO:] * inv_cnt - u[:, lanes])
    r = jnp.concatenate(rs, axis=1).astype(BF16)
    y = jnp.dot(r, wbd_ref[...], preferred_element_type=F32) * pscale_ref[...]
    yp_ref[...] = y.astype(BF16)
    uext_ref[0:POOL_HALO, :] = uext_ref[tm:tm + POOL_HALO, :]

    scale = HEAD_DIM ** -0.5 * LOG2_E
    for idx, (kind, c0) in enumerate(sections):
        zsec = pending
        if idx + 1 < len(sections):
            pending = section_dot(*sections[idx + 1])
        for jj in range(sec_w // LANES):
            zp = zsec[:, jj * LANES:(jj + 1) * LANES]
            for e in range(2):
                h = 2 * (c0 // LANES + jj) + e
                if kind == 0:
                    qa_ref[h] = jnp.where(data_lanes[e], zp * scale, aq).T.astype(BF16)
                elif kind == 1:
                    ka_ref[h] = jnp.where(data_lanes[e], zp, ak * mk[8 + h:9 + h]).astype(BF16)
                else:
                    v_t = jnp.where(data_lanes[e], zp, 1.0).T
                    r0 = 0 if e == 0 else LANES - VT_ROWS
                    va_ref[h] = v_t[r0:r0 + VT_ROWS].astype(BF16)


def _aug_lane_tables(n_heads):
    half = LANES // 2
    masks = np.zeros((8 + n_heads, LANES), np.float32)
    head_of_lane = np.full((LANES,), -1, np.int64)
    masks[6, :half] = 1.0
    masks[7, half:] = 1.0
    for h in range(n_heads):
        j, e = divmod(h, 2)
        base = half * (1 - e) + AUG_W * j
        for pos in range(AUG_W):
            masks[pos, base + pos] = 1.0
            masks[8 + h, base + pos] = 1.0
            head_of_lane[base + pos] = h
    return masks, head_of_lane


def _mix_in(x1, w_uqkv, wf_rep, bf_rep, wbd, pscale, masks, n_heads):
    bsz, seq, d = x1.shape
    tm = TM_MIX
    pool_w = wbd.shape[0]
    hspec = pl.BlockSpec((None, n_heads, tm, LANES), lambda b, s: (b, 0, s, 0))
    hshape = jax.ShapeDtypeStruct((bsz, n_heads, seq, LANES), BF16)
    tspec = pl.BlockSpec((None, n_heads, None, LANES, tm), lambda b, s: (b, 0, s, 0, 0))
    tshape = jax.ShapeDtypeStruct((bsz, n_heads, seq // tm, LANES, tm), BF16)
    vspec = pl.BlockSpec((None, n_heads, None, VT_ROWS, tm), lambda b, s: (b, 0, s, 0, 0))
    vshape = jax.ShapeDtypeStruct((bsz, n_heads, seq // tm, VT_ROWS, tm), BF16)
    return pl.pallas_call(
        functools.partial(_mix_in_kernel, tm=tm, n_heads=n_heads, pool_w=pool_w),
        grid=(bsz, seq // tm),
        in_specs=[pl.BlockSpec((None, tm, d), lambda b, s: (b, s, 0)),
                  _const_spec(w_uqkv.shape), _const_spec(wf_rep.shape), _const_spec(bf_rep.shape),
                  _const_spec(wbd.shape), _const_spec(pscale.shape), _const_spec(masks.shape)],
        out_specs=[tspec, hspec, vspec,
                   pl.BlockSpec((None, tm, pool_w), lambda b, s: (b, s, 0))],
        out_shape=[tshape, hshape, vshape, jax.ShapeDtypeStruct((bsz, seq, pool_w), BF16)],
        scratch_shapes=[pltpu.VMEM((POOL_HALO + tm, pool_w), F32), pltpu.VMEM((1, LANES), F32)],
        compiler_params=pltpu.CompilerParams(dimension_semantics=("arbitrary", "arbitrary"),
                                             vmem_limit_bytes=VMEM_LIMIT),
        name="mix_in",
    )(x1, w_uqkv, wf_rep, bf_rep, wbd, pscale, masks)


def _attn_kernel(qt_ref, ka_ref, vt_ref, o_ref, m_ref, acc_ref, *, tq, hps):
    qi = pl.program_id(2)

    qw = ATTN_STRIP
    strips = [slice(c, c + qw) for c in range(0, tq, qw)]

    def scores_t(e, kb, cols, nkeys):
        k0 = pl.multiple_of(kb * tq, tq)
        return jnp.dot(ka_ref[e, pl.ds(k0, nkeys), :], qt_ref[e, :, cols], preferred_element_type=F32)

    def blocks(kbs, diagonal=False):
        units = [(kb, e, cols, cols.stop if diagonal else tq)
                 for kb in kbs for e in range(hps) for cols in strips]
        ss, ps = {}, {}
        for step in range(len(units) + QK_LOOKAHEAD + PV_LAG):
            if step < len(units):
                kb, e, cols, nkeys = units[step]
                ss[step] = scores_t(e, kb, cols, nkeys)
            i = step - QK_LOOKAHEAD
            if 0 <= i < len(units):
                kb, e, cols, nkeys = units[i]
                s = ss.pop(i)
                if diagonal:
                    key_idx = lax.broadcasted_iota(jnp.int32, (nkeys, qw), 0)
                    query_idx = lax.broadcasted_iota(jnp.int32, (nkeys, qw), 1) + cols.start
                    s = jnp.where(key_idx <= query_idx, s, -jnp.inf)
                    m_new = jnp.max(s, axis=0, keepdims=True)
                    rescale = None
                else:
                    m_old = m_ref[e, :, cols]
                    m_new = jnp.maximum(m_old, jnp.max(s, axis=0, keepdims=True))
                    rescale = jnp.exp2(m_old - m_new)
                m_ref[e, :, cols] = m_new
                ps[i] = (jnp.exp2(s - m_new).astype(BF16), rescale)
            i = step - QK_LOOKAHEAD - PV_LAG
            if 0 <= i < len(units):
                kb, e, cols, nkeys = units[i]
                p_t, rescale = ps.pop(i)
                pv = jnp.dot(vt_ref[e, kb, :, :nkeys], p_t, preferred_element_type=F32)
                acc_ref[e, :, cols] = pv if diagonal else rescale * acc_ref[e, :, cols] + pv

    blocks([qi], diagonal=True)

    def body(i, carry):
        blocks([i * KV_UNROLL + r for r in range(KV_UNROLL)])
        return carry

    n_full = qi // KV_UNROLL
    lax.fori_loop(0, n_full, body, 0)

    def tail(kb, carry):
        blocks([kb])
        return carry

    lax.fori_loop(n_full * KV_UNROLL, qi, tail, 0)

    outs = []
    for e in range(hps):
        acc = acc_ref[e]
        if e % 2 == 0:
            outs.append(acc[:HEAD_DIM] / acc[HEAD_DIM:HEAD_DIM + 1])
        else:
            outs.append(acc[VT_ONES:] / acc[0:1])
    o_ref[...] = jnp.concatenate(outs, axis=0).T.astype(BF16)


def _attention(qt, ka, vt):
    bsz, n_heads, n_blk, _, tq = qt.shape
    seq = n_blk * tq
    hps = HEADS_PER_STEP
    return pl.pallas_call(
        functools.partial(_attn_kernel, tq=tq, hps=hps),
        grid=(bsz, n_heads // hps, n_blk),
        in_specs=[pl.BlockSpec((None, hps, None, LANES, tq), lambda b, j, i: (b, j, i, 0, 0)),
                  pl.BlockSpec((None, hps, seq, LANES), lambda b, j, i: (b, j, 0, 0)),
                  pl.BlockSpec((None, hps, n_blk, VT_ROWS, tq), lambda b, j, i: (b, j, 0, 0, 0))],
        out_specs=pl.BlockSpec((None, tq, hps * HEAD_DIM), lambda b, j, i: (b, i, j)),
        out_shape=jax.ShapeDtypeStruct((bsz, seq, n_heads * HEAD_DIM), BF16),
        scratch_shapes=[pltpu.VMEM((hps, 1, tq), F32), pltpu.VMEM((hps, VT_ROWS, tq), F32)],
        compiler_params=pltpu.CompilerParams(dimension_semantics=("arbitrary",) * 3,
                                             vmem_limit_bytes=VMEM_LIMIT),
        name="fox_attention",
    )(qt, ka, vt)


def _mix_out_kernel(x_ref, o_ref, yp_ref, wgl_ref, bg_ref, wpu_ref, wau_ref, wo_ref, g_ref, b_ref,
                    out_ref, *, alpha):
    x = x_ref[...]
    d = x.shape[1]
    gates = _sigmoid(jnp.dot(x.astype(BF16), wgl_ref[...], preferred_element_type=F32) + bg_ref[...])
    y_pool = jnp.dot(yp_ref[...], wpu_ref[...], preferred_element_type=F32)
    y_att = jnp.dot(o_ref[...], wau_ref[...], preferred_element_type=F32)
    merged = gates[:, :d] * y_pool + gates[:, d:] * y_att
    y = jnp.dot(merged.astype(BF16), wo_ref[...], preferred_element_type=F32)
    out_ref[...] = _layer_norm(alpha * x + y, g_ref[...], b_ref[...])


def _mix_out(x1, o, yp, wgl, bg, wpu, wau, wo, g, b, alpha):
    m, d = x1.shape
    tm = TM_MIX_OUT
    row = lambda i: (i, 0)
    consts = [wgl, bg, wpu, wau, wo, g, b]
    return pl.pallas_call(
        functools.partial(_mix_out_kernel, alpha=alpha),
        grid=(m // tm,),
        in_specs=[pl.BlockSpec((tm, d), row), pl.BlockSpec((tm, o.shape[1]), row),
                  pl.BlockSpec((tm, yp.shape[1]), row)] + [_const_spec(c.shape) for c in consts],
        out_specs=pl.BlockSpec((tm, d), row),
        out_shape=jax.ShapeDtypeStruct((m, d), F32),
        compiler_params=pltpu.CompilerParams(dimension_semantics=("arbitrary",),
                                             vmem_limit_bytes=VMEM_LIMIT),
        name="mix_out",
    )(x1, o, yp, *consts)


def _block_diag(w_group):
    n, c, _ = w_group.shape
    out = jnp.zeros((n * c, n * c), w_group.dtype)
    for gi in range(n):
        out = out.at[gi * c:(gi + 1) * c, gi * c:(gi + 1) * c].set(w_group[gi])
    return out


def kernel(x, p, ffn1_w_gate, ffn1_w_up, ffn1_w_down, ln1_g, ln1_b, mix_w_in, mix_b_f, mix_b_gate, pool_w_group, pool_scale, pool_w_up, att_w_up, mix_w_out, ln2_g, ln2_b, ffn2_w_gate, ffn2_w_up, ffn2_w_down, ln3_g, ln3_b, ple_w_proj, ple_w_gate, ln4_g, ln4_b):
    bsz, seq, d = x.shape
    depth = p.shape[0]
    n_heads = mix_b_f.shape[1]
    pool_w = pool_scale.shape[1]
    att_w = n_heads * HEAD_DIM
    alpha = (2 * depth) ** 0.25
    masks_np, head_of_lane = _aug_lane_tables(n_heads)
    masks = jnp.asarray(masks_np)
    lane_sel = jnp.asarray(np.maximum(head_of_lane, 0))
    lane_on = jnp.asarray((head_of_lane >= 0).astype(np.float32))
    row2 = lambda v: v.reshape(1, -1)

    xf = x.reshape(bsz * seq, d)
    for i in range(depth):
        xf = _ffn_ln(xf, ffn1_w_gate[i].astype(BF16), ffn1_w_up[i].astype(BF16), ffn1_w_down[i].astype(BF16),
                     row2(ln1_g[i]), row2(ln1_b[i]), alpha)

        w_in = mix_w_in[i]
        qkv_end = pool_w + 3 * att_w
        w_f = w_in[:, qkv_end:qkv_end + n_heads]
        wf_rep = (w_f[:, lane_sel] * lane_on).astype(BF16)
        bf_rep = row2(mix_b_f[i][lane_sel] * lane_on)
        qa, ka, va, yp = _mix_in(xf.reshape(bsz, seq, d), w_in[:, :qkv_end].astype(BF16), wf_rep, bf_rep,
                                 _block_diag(pool_w_group[i]).astype(BF16), row2(pool_scale[i]), masks, n_heads)
        o = _attention(qa, ka, va)
        xf = _mix_out(xf, o.reshape(bsz * seq, att_w), yp.reshape(bsz * seq, pool_w),
                      w_in[:, qkv_end + n_heads:].astype(BF16), row2(mix_b_gate[i]),
                      pool_w_up[i].astype(BF16), att_w_up[i].astype(BF16), mix_w_out[i].astype(BF16),
                      row2(ln2_g[i]), row2(ln2_b[i]), alpha)

        xf = _ffn_ln(xf, ffn2_w_gate[i].astype(BF16), ffn2_w_up[i].astype(BF16), ffn2_w_down[i].astype(BF16),
                     row2(ln3_g[i]), row2(ln3_b[i]), alpha,
                     ple=(p[i].reshape(bsz * seq, -1), ple_w_proj[i].astype(BF16), ple_w_gate[i].astype(BF16),
                          row2(ln4_g[i]), row2(ln4_b[i])))
    return xf.reshape(bsz, seq, d)
```

```python
import functools

import jax
import jax.numpy as jnp
import numpy as np
from jax import lax
from jax.experimental import pallas as pl
from jax.experimental.pallas import tpu as pltpu

F32 = jnp.float32
BF16 = jnp.bfloat16

LN_EPS = 1e-5
POOL_WINDOWS = (2, 4, 8, 16)
HEAD_DIM = 64
LANES = 128
POOL_HALO = 16
VT_ONES = 16
VT_ROWS = HEAD_DIM + VT_ONES
AUG_W = 6
V7X_VMEM_BYTES = 64 * 1024 * 1024
VMEM_LIMIT = V7X_VMEM_BYTES * 7 // 8

FFN_CHUNK = 256
TM_FFN = 1024
FFN_SUB = 256
TM_MIX_OUT = 1024
MIX_OUT_SUB = 256
TM_MIX = 512
CUMSUM_BLOCK = 256
LOG2_E = float(np.log2(np.e))
HEADS_PER_STEP = 8
ATTN_STRIP = 256
KV_UNROLL = 2
QK_LOOKAHEAD = 4
PV_LAG = 0


def _layer_norm(y, g, b):
    mu = jnp.mean(y, axis=-1, keepdims=True)
    yc = y - mu
    var = jnp.mean(yc * yc, axis=-1, keepdims=True)
    return yc * lax.rsqrt(var + LN_EPS) * g + b


def _sigmoid(x):
    return 1.0 / (1.0 + jnp.exp(-x))


def _const_spec(shape):
    nd = len(shape)
    return pl.BlockSpec(shape, lambda *_: (0,) * nd, pipeline_mode=pl.Buffered(1))


def _ffn_kernel(*refs, alpha, n_chunks, with_ple):
    if with_ple:
        (x_ref, wg_ref, wu_ref, wd_ref, g_ref, b_ref, p_ref, wp_ref, wpg_ref, g2_ref, b2_ref,
         o_ref, h_ref) = refs
    else:
        x_ref, wg_ref, wu_ref, wd_ref, g_ref, b_ref, o_ref, h_ref = refs
    fc = FFN_CHUNK
    sub = h_ref.shape[1]

    def finish(rows, y):
        x = _layer_norm(alpha * x_ref[rows, :] + 0.5 * y, g_ref[...], b_ref[...])
        if with_ple:
            proj = jnp.dot(p_ref[rows, :].astype(BF16), wp_ref[...], preferred_element_type=F32)
            gate = _sigmoid(jnp.dot(x.astype(BF16), wpg_ref[...], preferred_element_type=F32))
            x = _layer_norm(alpha * x + proj * gate, g2_ref[...], b2_ref[...])
        o_ref[rows, :] = x

    pending = None
    for s in range(x_ref.shape[0] // sub):
        rows = slice(s * sub, (s + 1) * sub)
        xb = x_ref[rows, :].astype(BF16)
        hbuf = h_ref.at[s % 2]
        for c in range(n_chunks):
            cols = slice(c * fc, (c + 1) * fc)
            g = jnp.dot(xb, wg_ref[:, cols], preferred_element_type=F32)
            u = jnp.dot(xb, wu_ref[:, cols], preferred_element_type=F32)
            hbuf[:, cols] = (g * _sigmoid(g) * u).astype(BF16)
            if c == 0 and pending is not None:
                finish(*pending)
        pending = (rows, jnp.dot(hbuf[...], wd_ref[...], preferred_element_type=F32))
    finish(*pending)


def _ffn_ln(x, wg, wu, wd, g, b, alpha, ple=None):
    m, d = x.shape
    dff = wd.shape[0]
    tm = TM_FFN
    row = lambda i: (i, 0)
    in_specs = [pl.BlockSpec((tm, d), row), _const_spec(wg.shape), _const_spec(wu.shape), _const_spec(wd.shape),
                _const_spec(g.shape), _const_spec(b.shape)]
    args = [x, wg, wu, wd, g, b]
    if ple is not None:
        p, wp, wpg, g2, b2 = ple
        in_specs += [pl.BlockSpec((tm, p.shape[1]), row), _const_spec(wp.shape), _const_spec(wpg.shape),
                     _const_spec(g2.shape), _const_spec(b2.shape)]
        args += [p, wp, wpg, g2, b2]
    return pl.pallas_call(
        functools.partial(_ffn_kernel, alpha=alpha, n_chunks=dff // FFN_CHUNK, with_ple=ple is not None),
        grid=(m // tm,),
        in_specs=in_specs,
        out_specs=pl.BlockSpec((tm, d), row),
        out_shape=jax.ShapeDtypeStruct((m, d), F32),
        scratch_shapes=[pltpu.VMEM((2, FFN_SUB, dff), BF16)],
        compiler_params=pltpu.CompilerParams(dimension_semantics=("arbitrary",),
                                             vmem_limit_bytes=VMEM_LIMIT),
        name="ffn_ln_ple" if ple is not None else "ffn_ln",
    )(*args)


def _split3(x):
    hi = x.astype(BF16).astype(F32)
    r = x - hi
    mid = r.astype(BF16).astype(F32)
    lo = (r - mid).astype(BF16).astype(F32)
    return hi, mid, lo


def _mix_in_kernel(x_ref, w_ref, wf_ref, bf_ref, wbd_ref, pscale_ref, masks_ref,
                   qa_ref, ka_ref, va_ref, yp_ref, uext_ref, fcarry_ref, *, tm, n_heads, pool_w):
    si = pl.program_id(1)

    @pl.when(si == 0)
    def _():
        uext_ref[0:POOL_HALO, :] = jnp.zeros((POOL_HALO, pool_w), F32)
        fcarry_ref[...] = jnp.zeros_like(fcarry_ref)

    xb = x_ref[...].astype(BF16)
    mk = masks_ref[...]

    att_w = n_heads * HEAD_DIM
    sec_w = 2 * LANES
    sections = [(kind, c0) for kind in (2, 0, 1) for c0 in range(0, att_w, sec_w)]

    def section_dot(kind, c0):
        col = pool_w + kind * att_w + c0
        return jnp.dot(xb, w_ref[:, col:col + sec_w], preferred_element_type=F32)

    fl = jnp.dot(xb, wf_ref[...], preferred_element_type=F32) + bf_ref[...]
    u = jnp.dot(xb, w_ref[:, :pool_w], preferred_element_type=F32)
    pending = section_dot(*sections[0])
    logf = jnp.minimum(fl, 0.0) - jnp.log1p(jnp.exp(-jnp.abs(fl)))
    cb = CUMSUM_BLOCK
    tri = (lax.broadcasted_iota(jnp.int32, (cb, cb), 0)
           >= lax.broadcasted_iota(jnp.int32, (cb, cb), 1)).astype(BF16)
    terms = [t.astype(BF16) for t in _split3(logf)]
    running = fcarry_ref[...]
    blocks = []
    for r0 in range(0, tm, cb):
        blk = running
        for t in terms:
            blk = blk + jnp.dot(tri, t[r0:r0 + cb], preferred_element_type=F32)
        blocks.append(blk)
        running = blk[cb - 1:cb, :]
    fcum = jnp.concatenate(blocks, axis=0)
    fcarry_ref[...] = running

    hi, mid, lo = _split3(fcum * LOG2_E)
    aq = hi * mk[0:1] + mid * mk[1:2] + lo * mk[2:3] + (mk[3:4] + mk[4:5] + mk[5:6])
    ak = (mk[0:1] + mk[1:2] + mk[2:3]) - (hi * mk[3:4] + mid * mk[4:5] + lo * mk[5:6])
    lo_half = lax.broadcasted_iota(jnp.int32, (tm, LANES), 1) < LANES // 2
    data_lanes = (lo_half, jnp.logical_not(lo_half))

    uext_ref[POOL_HALO:POOL_HALO + tm, :] = u
    t_glob = si * tm + lax.broadcasted_iota(jnp.int32, (tm, 1), 0)
    pg = pool_w // len(POOL_WINDOWS)
    rs = []
    for gi, w in enumerate(POOL_WINDOWS):
        lanes = slice(gi * pg, (gi + 1) * pg)
        wsum = uext_ref[:, lanes]
        span = 1
        while span < w:
            wsum = wsum + pltpu.roll(wsum, span, 0)
            span *= 2
        inv_cnt = 1.0 / jnp.minimum(t_glob + 1, w).astype(F32)
        rs.append(wsum[POOL_HALO:] * inv_cnt - u[:, lanes])
    r = jnp.concatenate(rs, axis=1).astype(BF16)
    y = jnp.dot(r, wbd_ref[...], preferred_element_type=F32) * pscale_ref[...]
    yp_ref[...] = y.astype(BF16)
    uext_ref[0:POOL_HALO, :] = uext_ref[tm:tm + POOL_HALO, :]

    scale = HEAD_DIM ** -0.5 * LOG2_E
    for idx, (kind, c0) in enumerate(sections):
        zsec = pending
        if idx + 1 < len(sections):
            pending = section_dot(*sections[idx + 1])
        for jj in range(sec_w // LANES):
            zp = zsec[:, jj * LANES:(jj + 1) * LANES]
            for e in range(2):
                h = 2 * (c0 // LANES + jj) + e
                if kind == 0:
                    qa_ref[h] = jnp.where(data_lanes[e], zp * scale, aq).T.astype(BF16)
                elif kind == 1:
                    ka_ref[h] = jnp.where(data_lanes[e], zp, ak * mk[8 + h:9 + h]).astype(BF16)
                else:
                    v_t = jnp.where(data_lanes[e], zp, 1.0).T
                    r0 = 0 if e == 0 else LANES - VT_ROWS
                    va_ref[h] = v_t[r0:r0 + VT_ROWS].astype(BF16)


def _aug_lane_tables(n_heads):
    half = LANES // 2
    masks = np.zeros((8 + n_heads, LANES), np.float32)
    head_of_lane = np.full((LANES,), -1, np.int64)
    masks[6, :half] = 1.0
    masks[7, half:] = 1.0
    for h in range(n_heads):
        j, e = divmod(h, 2)
        base = half * (1 - e) + AUG_W * j
        for pos in range(AUG_W):
            masks[pos, base + pos] = 1.0
            masks[8 + h, base + pos] = 1.0
            head_of_lane[base + pos] = h
    return masks, head_of_lane


def _mix_in(x1, w_uqkv, wf_rep, bf_rep, wbd, pscale, masks, n_heads):
    bsz, seq, d = x1.shape
    tm = TM_MIX
    pool_w = wbd.shape[0]
    hspec = pl.BlockSpec((None, n_heads, tm, LANES), lambda b, s: (b, 0, s, 0))
    hshape = jax.ShapeDtypeStruct((bsz, n_heads, seq, LANES), BF16)
    tspec = pl.BlockSpec((None, n_heads, None, LANES, tm), lambda b, s: (b, 0, s, 0, 0))
    tshape = jax.ShapeDtypeStruct((bsz, n_heads, seq // tm, LANES, tm), BF16)
    vspec = pl.BlockSpec((None, n_heads, None, VT_ROWS, tm), lambda b, s: (b, 0, s, 0, 0))
    vshape = jax.ShapeDtypeStruct((bsz, n_heads, seq // tm, VT_ROWS, tm), BF16)
    return pl.pallas_call(
        functools.partial(_mix_in_kernel, tm=tm, n_heads=n_heads, pool_w=pool_w),
        grid=(bsz, seq // tm),
        in_specs=[pl.BlockSpec((None, tm, d), lambda b, s: (b, s, 0)),
                  _const_spec(w_uqkv.shape), _const_spec(wf_rep.shape), _const_spec(bf_rep.shape),
                  _const_spec(wbd.shape), _const_spec(pscale.shape), _const_spec(masks.shape)],
        out_specs=[tspec, hspec, vspec,
                   pl.BlockSpec((None, tm, pool_w), lambda b, s: (b, s, 0))],
        out_shape=[tshape, hshape, vshape, jax.ShapeDtypeStruct((bsz, seq, pool_w), BF16)],
        scratch_shapes=[pltpu.VMEM((POOL_HALO + tm, pool_w), F32), pltpu.VMEM((1, LANES), F32)],
        compiler_params=pltpu.CompilerParams(dimension_semantics=("arbitrary", "arbitrary"),
                                             vmem_limit_bytes=VMEM_LIMIT),
        name="mix_in",
    )(x1, w_uqkv, wf_rep, bf_rep, wbd, pscale, masks)


def _attn_kernel(qt_ref, ka_ref, vt_ref, o_ref, m_ref, acc_ref, *, tq, hps):
    qi = pl.program_id(2)

    qw = ATTN_STRIP
    strips = [slice(c, c + qw) for c in range(0, tq, qw)]

    def scores_t(e, kb, cols, nkeys):
        k0 = pl.multiple_of(kb * tq, tq)
        return jnp.dot(ka_ref[e, pl.ds(k0, nkeys), :], qt_ref[e, :, cols], preferred_element_type=F32)

    def blocks(kbs, diagonal=False):
        units = [(kb, e, cols, cols.stop if diagonal else tq)
                 for kb in kbs for e in range(hps) for cols in strips]
        ss, ps = {}, {}
        for step in range(len(units) + QK_LOOKAHEAD + PV_LAG):
            if step < len(units):
                kb, e, cols, nkeys = units[step]
                ss[step] = scores_t(e, kb, cols, nkeys)
            i = step - QK_LOOKAHEAD
            if 0 <= i < len(units):
                kb, e, cols, nkeys = units[i]
                s = ss.pop(i)
                if diagonal:
                    key_idx = lax.broadcasted_iota(jnp.int32, (nkeys, qw), 0)
                    query_idx = lax.broadcasted_iota(jnp.int32, (nkeys, qw), 1) + cols.start
                    s = jnp.where(key_idx <= query_idx, s, -jnp.inf)
                    m_new = jnp.max(s, axis=0, keepdims=True)
                    rescale = None
                else:
                    m_old = m_ref[e, :, cols]
                    m_new = jnp.maximum(m_old, jnp.max(s, axis=0, keepdims=True))
                    rescale = jnp.exp2(m_old - m_new)
                m_ref[e, :, cols] = m_new
                ps[i] = (jnp.exp2(s - m_new).astype(BF16), rescale)
            i = step - QK_LOOKAHEAD - PV_LAG
            if 0 <= i < len(units):
                kb, e, cols, nkeys = units[i]
                p_t, rescale = ps.pop(i)
                pv = jnp.dot(vt_ref[e, kb, :, :nkeys], p_t, preferred_element_type=F32)
                acc_ref[e, :, cols] = pv if diagonal else rescale * acc_ref[e, :, cols] + pv

    blocks([qi], diagonal=True)

    def body(i, carry):
        blocks([i * KV_UNROLL + r for r in range(KV_UNROLL)])
        return carry

    n_full = qi // KV_UNROLL
    lax.fori_loop(0, n_full, body, 0)

    def tail(kb, carry):
        blocks([kb])
        return carry

    lax.fori_loop(n_full * KV_UNROLL, qi, tail, 0)

    outs = []
    for e in range(hps):
        acc = acc_ref[e]
        if e % 2 == 0:
            outs.append(acc[:HEAD_DIM] / acc[HEAD_DIM:HEAD_DIM + 1])
        else:
            outs.append(acc[VT_ONES:] / acc[0:1])
    o_ref[...] = jnp.concatenate(outs, axis=0).T.astype(BF16)


def _attention(qt, ka, vt):
    bsz, n_heads, n_blk, _, tq = qt.shape
    seq = n_blk * tq
    hps = HEADS_PER_STEP
    return pl.pallas_call(
        functools.partial(_attn_kernel, tq=tq, hps=hps),
        grid=(bsz, n_heads // hps, n_blk),
        in_specs=[pl.BlockSpec((None, hps, None, LANES, tq), lambda b, j, i: (b, j, i, 0, 0)),
                  pl.BlockSpec((None, hps, seq, LANES), lambda b, j, i: (b, j, 0, 0)),
                  pl.BlockSpec((None, hps, n_blk, VT_ROWS, tq), lambda b, j, i: (b, j, 0, 0, 0))],
        out_specs=pl.BlockSpec((None, tq, hps * HEAD_DIM), lambda b, j, i: (b, i, j)),
        out_shape=jax.ShapeDtypeStruct((bsz, seq, n_heads * HEAD_DIM), BF16),
        scratch_shapes=[pltpu.VMEM((hps, 1, tq), F32), pltpu.VMEM((hps, VT_ROWS, tq), F32)],
        compiler_params=pltpu.CompilerParams(dimension_semantics=("arbitrary",) * 3,
                                             vmem_limit_bytes=VMEM_LIMIT),
        name="fox_attention",
    )(qt, ka, vt)


def _mix_out_kernel(x_ref, o_ref, yp_ref, wgl_ref, bg_ref, wpu_ref, wau_ref, wo_ref, g_ref, b_ref,
                    out_ref, *, alpha):
    d = x_ref.shape[1]
    sub = MIX_OUT_SUB

    def finish(rows, y):
        out_ref[rows, :] = _layer_norm(alpha * x_ref[rows, :] + y, g_ref[...], b_ref[...])

    pending = None
    for s in range(x_ref.shape[0] // sub):
        rows = slice(s * sub, (s + 1) * sub)
        glog = jnp.dot(x_ref[rows, :].astype(BF16), wgl_ref[...], preferred_element_type=F32)
        if pending is not None:
            finish(*pending)
        gates = _sigmoid(glog + bg_ref[...])
        y_pool = jnp.dot(yp_ref[rows, :], wpu_ref[...], preferred_element_type=F32)
        y_att = jnp.dot(o_ref[rows, :], wau_ref[...], preferred_element_type=F32)
        merged = gates[:, :d] * y_pool + gates[:, d:] * y_att
        pending = (rows, jnp.dot(merged.astype(BF16), wo_ref[...], preferred_element_type=F32))
    finish(*pending)


def _mix_out(x1, o, yp, wgl, bg, wpu, wau, wo, g, b, alpha):
    m, d = x1.shape
    tm = TM_MIX_OUT
    row = lambda i: (i, 0)
    consts = [wgl, bg, wpu, wau, wo, g, b]
    return pl.pallas_call(
        functools.partial(_mix_out_kernel, alpha=alpha),
        grid=(m // tm,),
        in_specs=[pl.BlockSpec((tm, d), row), pl.BlockSpec((tm, o.shape[1]), row),
                  pl.BlockSpec((tm, yp.shape[1]), row)] + [_const_spec(c.shape) for c in consts],
        out_specs=pl.BlockSpec((tm, d), row),
        out_shape=jax.ShapeDtypeStruct((m, d), F32),
        compiler_params=pltpu.CompilerParams(dimension_semantics=("arbitrary",),
                                             vmem_limit_bytes=VMEM_LIMIT),
        name="mix_out",
    )(x1, o, yp, *consts)


def _block_diag(w_group):
    n, c, _ = w_group.shape
    out = jnp.zeros((n * c, n * c), w_group.dtype)
    for gi in range(n):
        out = out.at[gi * c:(gi + 1) * c, gi * c:(gi + 1) * c].set(w_group[gi])
    return out


def kernel(x, p, ffn1_w_gate, ffn1_w_up, ffn1_w_down, ln1_g, ln1_b, mix_w_in, mix_b_f, mix_b_gate, pool_w_group, pool_scale, pool_w_up, att_w_up, mix_w_out, ln2_g, ln2_b, ffn2_w_gate, ffn2_w_up, ffn2_w_down, ln3_g, ln3_b, ple_w_proj, ple_w_gate, ln4_g, ln4_b):
    bsz, seq, d = x.shape
    depth = p.shape[0]
    n_heads = mix_b_f.shape[1]
    pool_w = pool_scale.shape[1]
    att_w = n_heads * HEAD_DIM
    alpha = (2 * depth) ** 0.25
    masks_np, head_of_lane = _aug_lane_tables(n_heads)
    masks = jnp.asarray(masks_np)
    lane_sel = jnp.asarray(np.maximum(head_of_lane, 0))
    lane_on = jnp.asarray((head_of_lane >= 0).astype(np.float32))
    row2 = lambda v: v.reshape(1, -1)

    xf = x.reshape(bsz * seq, d)
    for i in range(depth):
        xf = _ffn_ln(xf, ffn1_w_gate[i].astype(BF16), ffn1_w_up[i].astype(BF16), ffn1_w_down[i].astype(BF16),
                     row2(ln1_g[i]), row2(ln1_b[i]), alpha)

        w_in = mix_w_in[i]
        qkv_end = pool_w + 3 * att_w
        w_f = w_in[:, qkv_end:qkv_end + n_heads]
        wf_rep = (w_f[:, lane_sel] * lane_on).astype(BF16)
        bf_rep = row2(mix_b_f[i][lane_sel] * lane_on)
        qa, ka, va, yp = _mix_in(xf.reshape(bsz, seq, d), w_in[:, :qkv_end].astype(BF16), wf_rep, bf_rep,
                                 _block_diag(pool_w_group[i]).astype(BF16), row2(pool_scale[i]), masks, n_heads)
        o = _attention(qa, ka, va)
        xf = _mix_out(xf, o.reshape(bsz * seq, att_w), yp.reshape(bsz * seq, pool_w),
                      w_in[:, qkv_end + n_heads:].astype(BF16), row2(mix_b_gate[i]),
                      pool_w_up[i].astype(BF16), att_w_up[i].astype(BF16), mix_w_out[i].astype(BF16),
                      row2(ln2_g[i]), row2(ln2_b[i]), alpha)

        xf = _ffn_ln(xf, ffn2_w_gate[i].astype(BF16), ffn2_w_up[i].astype(BF16), ffn2_w_down[i].astype(BF16),
                     row2(ln3_g[i]), row2(ln3_b[i]), alpha,
                     ple=(p[i].reshape(bsz * seq, -1), ple_w_proj[i].astype(BF16), ple_w_gate[i].astype(BF16),
                          row2(ln4_g[i]), row2(ln4_b[i])))
    return xf.reshape(bsz, seq, d)
```

```python
import functools

import jax
import jax.numpy as jnp
import numpy as np
from jax import lax
from jax.experimental import pallas as pl
from jax.experimental.pallas import tpu as pltpu

F32 = jnp.float32
BF16 = jnp.bfloat16

LN_EPS = 1e-5
POOL_WINDOWS = (2, 4, 8, 16)
HEAD_DIM = 64
LANES = 128
POOL_HALO = 16
VT_ONES = 16
VT_ROWS = HEAD_DIM + VT_ONES
AUG_W = 6
V7X_VMEM_BYTES = 64 * 1024 * 1024
VMEM_LIMIT = V7X_VMEM_BYTES * 7 // 8

FFN_CHUNK = 256
TM_FFN = 1024
FFN_SUB = 256
TM_MIX_OUT = 1024
MIX_OUT_SUB = 256
TM_MIX = 512
CUMSUM_BLOCK = 256
LOG2_E = float(np.log2(np.e))
HEADS_PER_STEP = 8
ATTN_STRIP = 256
KV_UNROLL = 2
QK_LOOKAHEAD = 4


def _layer_norm(y, g, b):
    mu = jnp.mean(y, axis=-1, keepdims=True)
    yc = y - mu
    var = jnp.mean(yc * yc, axis=-1, keepdims=True)
    return yc * lax.rsqrt(var + LN_EPS) * g + b


def _sigmoid(x):
    return 1.0 / (1.0 + jnp.exp(-x))


def _const_spec(shape):
    nd = len(shape)
    return pl.BlockSpec(shape, lambda *_: (0,) * nd, pipeline_mode=pl.Buffered(1))


def _ffn_kernel(*refs, alpha, n_chunks, with_ple):
    if with_ple:
        (x_ref, wg_ref, wu_ref, wd_ref, g_ref, b_ref, p_ref, wp_ref, wpg_ref, g2_ref, b2_ref,
         o_ref, h_ref) = refs
    else:
        x_ref, wg_ref, wu_ref, wd_ref, g_ref, b_ref, o_ref, h_ref = refs
    fc = FFN_CHUNK
    sub = h_ref.shape[1]

    def finish(rows, y):
        x = _layer_norm(alpha * x_ref[rows, :] + 0.5 * y, g_ref[...], b_ref[...])
        if with_ple:
            proj = jnp.dot(p_ref[rows, :].astype(BF16), wp_ref[...], preferred_element_type=F32)
            gate = _sigmoid(jnp.dot(x.astype(BF16), wpg_ref[...], preferred_element_type=F32))
            x = _layer_norm(alpha * x + proj * gate, g2_ref[...], b2_ref[...])
        o_ref[rows, :] = x

    pending = None
    for s in range(x_ref.shape[0] // sub):
        rows = slice(s * sub, (s + 1) * sub)
        xb = x_ref[rows, :].astype(BF16)
        hbuf = h_ref.at[s % 2]
        for c in range(n_chunks):
            cols = slice(c * fc, (c + 1) * fc)
            g = jnp.dot(xb, wg_ref[:, cols], preferred_element_type=F32)
            u = jnp.dot(xb, wu_ref[:, cols], preferred_element_type=F32)
            hbuf[:, cols] = (g * _sigmoid(g) * u).astype(BF16)
            if c == 0 and pending is not None:
                finish(*pending)
        pending = (rows, jnp.dot(hbuf[...], wd_ref[...], preferred_element_type=F32))
    finish(*pending)


def _ffn_ln(x, wg, wu, wd, g, b, alpha, ple=None):
    m, d = x.shape
    dff = wd.shape[0]
    tm = TM_FFN
    row = lambda i: (i, 0)
    in_specs = [pl.BlockSpec((tm, d), row), _const_spec(wg.shape), _const_spec(wu.shape), _const_spec(wd.shape),
                _const_spec(g.shape), _const_spec(b.shape)]
    args = [x, wg, wu, wd, g, b]
    if ple is not None:
        p, wp, wpg, g2, b2 = ple
        in_specs += [pl.BlockSpec((tm, p.shape[1]), row), _const_spec(wp.shape), _const_spec(wpg.shape),
                     _const_spec(g2.shape), _const_spec(b2.shape)]
        args += [p, wp, wpg, g2, b2]
    return pl.pallas_call(
        functools.partial(_ffn_kernel, alpha=alpha, n_chunks=dff // FFN_CHUNK, with_ple=ple is not None),
        grid=(m // tm,),
        in_specs=in_specs,
        out_specs=pl.BlockSpec((tm, d), row),
        out_shape=jax.ShapeDtypeStruct((m, d), F32),
        scratch_shapes=[pltpu.VMEM((2, FFN_SUB, dff), BF16)],
        compiler_params=pltpu.CompilerParams(dimension_semantics=("arbitrary",),
                                             vmem_limit_bytes=VMEM_LIMIT),
        name="ffn_ln_ple" if ple is not None else "ffn_ln",
    )(*args)


def _split3(x):
    hi = x.astype(BF16).astype(F32)
    r = x - hi
    mid = r.astype(BF16).astype(F32)
    lo = (r - mid).astype(BF16).astype(F32)
    return hi, mid, lo


def _mix_in_kernel(x_ref, w_ref, wf_ref, bf_ref, wbd_ref, pscale_ref, masks_ref,
                   qa_ref, ka_ref, va_ref, yp_ref, uext_ref, fcarry_ref, *, tm, n_heads, pool_w):
    si = pl.program_id(1)

    @pl.when(si == 0)
    def _():
        uext_ref[0:POOL_HALO, :] = jnp.zeros((POOL_HALO, pool_w), F32)
        fcarry_ref[...] = jnp.zeros_like(fcarry_ref)

    xb = x_ref[...].astype(BF16)
    mk = masks_ref[...]

    att_w = n_heads * HEAD_DIM
    sec_w = 2 * LANES
    sections = [(kind, c0) for kind in (2, 0, 1) for c0 in range(0, att_w, sec_w)]

    def section_dot(kind, c0):
        col = pool_w + kind * att_w + c0
        return jnp.dot(xb, w_ref[:, col:col + sec_w], preferred_element_type=F32)

    fl = jnp.dot(xb, wf_ref[...], preferred_element_type=F32) + bf_ref[...]
    u = jnp.dot(xb, w_ref[:, :pool_w], preferred_element_type=F32)
    pending = section_dot(*sections[0])

    logf = jnp.minimum(fl, 0.0) - jnp.log1p(jnp.exp(-jnp.abs(fl)))
    cb = CUMSUM_BLOCK
    tri = (lax.broadcasted_iota(jnp.int32, (cb, cb), 0)
           >= lax.broadcasted_iota(jnp.int32, (cb, cb), 1)).astype(BF16)
    terms = [t.astype(BF16) for t in _split3(logf)]
    running = fcarry_ref[...]
    blocks = []
    for r0 in range(0, tm, cb):
        blk = running
        for t in terms:
            blk = blk + jnp.dot(tri, t[r0:r0 + cb], preferred_element_type=F32)
        blocks.append(blk)
        running = blk[cb - 1:cb, :]
    fcum = jnp.concatenate(blocks, axis=0)
    fcarry_ref[...] = running

    hi, mid, lo = _split3(fcum * LOG2_E)
    aq = hi * mk[0:1] + mid * mk[1:2] + lo * mk[2:3] + (mk[3:4] + mk[4:5] + mk[5:6])
    ak = (mk[0:1] + mk[1:2] + mk[2:3]) - (hi * mk[3:4] + mid * mk[4:5] + lo * mk[5:6])
    lo_half = lax.broadcasted_iota(jnp.int32, (tm, LANES), 1) < LANES // 2
    data_lanes = (lo_half, jnp.logical_not(lo_half))

    uext_ref[POOL_HALO:POOL_HALO + tm, :] = u
    t_glob = si * tm + lax.broadcasted_iota(jnp.int32, (tm, 1), 0)
    pg = pool_w // len(POOL_WINDOWS)
    rs = []
    for gi, w in enumerate(POOL_WINDOWS):
        lanes = slice(gi * pg, (gi + 1) * pg)
        wsum = uext_ref[:, lanes]
        span = 1
        while span < w:
            wsum = wsum + pltpu.roll(wsum, span, 0)
            span *= 2
        inv_cnt = 1.0 / jnp.minimum(t_glob + 1, w).astype(F32)
        rs.append(wsum[POOL_HALO:] * inv_cnt - u[:, lanes])
    r = jnp.concatenate(rs, axis=1).astype(BF16)
    y = jnp.dot(r, wbd_ref[...], preferred_element_type=F32) * pscale_ref[...]
    yp_ref[...] = y.astype(BF16)
    uext_ref[0:POOL_HALO, :] = uext_ref[tm:tm + POOL_HALO, :]

    scale = HEAD_DIM ** -0.5 * LOG2_E
    for idx, (kind, c0) in enumerate(sections):
        zsec = pending
        if idx + 1 < len(sections):
            pending = section_dot(*sections[idx + 1])
        for jj in range(sec_w // LANES):
            zp = zsec[:, jj * LANES:(jj + 1) * LANES]
            for e in range(2):
                h = 2 * (c0 // LANES + jj) + e
                if kind == 0:
                    qa_ref[h] = jnp.where(data_lanes[e], zp * scale, aq).T.astype(BF16)
                elif kind == 1:
                    ka_ref[h] = jnp.where(data_lanes[e], zp, ak * mk[AUG_W + h:AUG_W + h + 1]).astype(BF16)
                else:
                    v_t = jnp.where(data_lanes[e], zp, 1.0).T
                    r0 = 0 if e == 0 else LANES - VT_ROWS
                    va_ref[h] = v_t[r0:r0 + VT_ROWS].astype(BF16)


def _aug_lane_tables(n_heads):
    half = LANES // 2
    masks = np.zeros((AUG_W + n_heads, LANES), np.float32)
    head_of_lane = np.full((LANES,), -1, np.int64)
    for h in range(n_heads):
        j, e = divmod(h, 2)
        base = half * (1 - e) + AUG_W * j
        for pos in range(AUG_W):
            masks[pos, base + pos] = 1.0
            masks[AUG_W + h, base + pos] = 1.0
            head_of_lane[base + pos] = h
    return masks, head_of_lane


def _mix_in(x1, w_uqkv, wf_rep, bf_rep, wbd, pscale, masks, n_heads):
    bsz, seq, d = x1.shape
    tm = TM_MIX
    pool_w = wbd.shape[0]
    hspec = pl.BlockSpec((None, n_heads, tm, LANES), lambda b, s: (b, 0, s, 0))
    hshape = jax.ShapeDtypeStruct((bsz, n_heads, seq, LANES), BF16)
    tspec = pl.BlockSpec((None, n_heads, None, LANES, tm), lambda b, s: (b, 0, s, 0, 0))
    tshape = jax.ShapeDtypeStruct((bsz, n_heads, seq // tm, LANES, tm), BF16)
    vspec = pl.BlockSpec((None, n_heads, None, VT_ROWS, tm), lambda b, s: (b, 0, s, 0, 0))
    vshape = jax.ShapeDtypeStruct((bsz, n_heads, seq // tm, VT_ROWS, tm), BF16)
    return pl.pallas_call(
        functools.partial(_mix_in_kernel, tm=tm, n_heads=n_heads, pool_w=pool_w),
        grid=(bsz, seq // tm),
        in_specs=[pl.BlockSpec((None, tm, d), lambda b, s: (b, s, 0)),
                  _const_spec(w_uqkv.shape), _const_spec(wf_rep.shape), _const_spec(bf_rep.shape),
                  _const_spec(wbd.shape), _const_spec(pscale.shape), _const_spec(masks.shape)],
        out_specs=[tspec, hspec, vspec,
                   pl.BlockSpec((None, tm, pool_w), lambda b, s: (b, s, 0))],
        out_shape=[tshape, hshape, vshape, jax.ShapeDtypeStruct((bsz, seq, pool_w), BF16)],
        scratch_shapes=[pltpu.VMEM((POOL_HALO + tm, pool_w), F32), pltpu.VMEM((1, LANES), F32)],
        compiler_params=pltpu.CompilerParams(dimension_semantics=("arbitrary", "arbitrary"),
                                             vmem_limit_bytes=VMEM_LIMIT),
        name="mix_in",
    )(x1, w_uqkv, wf_rep, bf_rep, wbd, pscale, masks)


def _attn_kernel(qt_ref, ka_ref, vt_ref, o_ref, m_ref, acc_ref, *, tq, hps):
    qi = pl.program_id(2)

    qw = ATTN_STRIP
    strips = [slice(c, c + qw) for c in range(0, tq, qw)]

    def scores_t(e, kb, cols, nkeys):
        k0 = pl.multiple_of(kb * tq, tq)
        return jnp.dot(ka_ref[e, pl.ds(k0, nkeys), :], qt_ref[e, :, cols], preferred_element_type=F32)

    def blocks(kbs, diagonal=False):
        units = [(kb, e, cols, cols.stop if diagonal else tq)
                 for kb in kbs for e in range(hps) for cols in strips]
        ss = {}
        for step in range(len(units) + QK_LOOKAHEAD):
            if step < len(units):
                kb, e, cols, nkeys = units[step]
                ss[step] = scores_t(e, kb, cols, nkeys)
            i = step - QK_LOOKAHEAD
            if 0 <= i < len(units):
                kb, e, cols, nkeys = units[i]
                s = ss.pop(i)
                if diagonal:
                    key_idx = lax.broadcasted_iota(jnp.int32, (nkeys, qw), 0)
                    query_idx = lax.broadcasted_iota(jnp.int32, (nkeys, qw), 1) + cols.start
                    s = jnp.where(key_idx <= query_idx, s, -jnp.inf)
                    m_new = jnp.max(s, axis=0, keepdims=True)
                else:
                    m_old = m_ref[e, :, cols]
                    m_new = jnp.maximum(m_old, jnp.max(s, axis=0, keepdims=True))
                    rescale = jnp.exp2(m_old - m_new)
                m_ref[e, :, cols] = m_new
                p_t = jnp.exp2(s - m_new).astype(BF16)
                pv = jnp.dot(vt_ref[e, kb, :, :nkeys], p_t, preferred_element_type=F32)
                acc_ref[e, :, cols] = pv if diagonal else rescale * acc_ref[e, :, cols] + pv

    blocks([qi], diagonal=True)

    def body(i, carry):
        blocks([i * KV_UNROLL + r for r in range(KV_UNROLL)])
        return carry

    n_full = qi // KV_UNROLL
    lax.fori_loop(0, n_full, body, 0)

    def tail(kb, carry):
        blocks([kb])
        return carry

    lax.fori_loop(n_full * KV_UNROLL, qi, tail, 0)

    outs = []
    for e in range(hps):
        acc = acc_ref[e]
        if e % 2 == 0:
            outs.append(acc[:HEAD_DIM] / acc[HEAD_DIM:HEAD_DIM + 1])
        else:
            outs.append(acc[VT_ONES:] / acc[0:1])
    o_ref[...] = jnp.concatenate(outs, axis=0).T.astype(BF16)


def _attention(qt, ka, vt):
    bsz, n_heads, n_blk, _, tq = qt.shape
    seq = n_blk * tq
    hps = HEADS_PER_STEP
    return pl.pallas_call(
        functools.partial(_attn_kernel, tq=tq, hps=hps),
        grid=(bsz, n_heads // hps, n_blk),
        in_specs=[pl.BlockSpec((None, hps, None, LANES, tq), lambda b, j, i: (b, j, i, 0, 0)),
                  pl.BlockSpec((None, hps, seq, LANES), lambda b, j, i: (b, j, 0, 0)),
                  pl.BlockSpec((None, hps, n_blk, VT_ROWS, tq), lambda b, j, i: (b, j, 0, 0, 0))],
        out_specs=pl.BlockSpec((None, tq, hps * HEAD_DIM), lambda b, j, i: (b, i, j)),
        out_shape=jax.ShapeDtypeStruct((bsz, seq, n_heads * HEAD_DIM), BF16),
        scratch_shapes=[pltpu.VMEM((hps, 1, tq), F32), pltpu.VMEM((hps, VT_ROWS, tq), F32)],
        compiler_params=pltpu.CompilerParams(dimension_semantics=("arbitrary",) * 3,
                                             vmem_limit_bytes=VMEM_LIMIT),
        name="fox_attention",
    )(qt, ka, vt)


def _mix_out_kernel(x_ref, o_ref, yp_ref, wgl_ref, bg_ref, wpu_ref, wau_ref, wo_ref, g_ref, b_ref,
                    out_ref, *, alpha):
    d = x_ref.shape[1]
    sub = MIX_OUT_SUB

    def finish(rows, y):
        out_ref[rows, :] = _layer_norm(alpha * x_ref[rows, :] + y, g_ref[...], b_ref[...])

    pending = None
    for s in range(x_ref.shape[0] // sub):
        rows = slice(s * sub, (s + 1) * sub)
        glog = jnp.dot(x_ref[rows, :].astype(BF16), wgl_ref[...], preferred_element_type=F32)
        if pending is not None:
            finish(*pending)
        gates = _sigmoid(glog + bg_ref[...])
        y_pool = jnp.dot(yp_ref[rows, :], wpu_ref[...], preferred_element_type=F32)
        y_att = jnp.dot(o_ref[rows, :], wau_ref[...], preferred_element_type=F32)
        merged = gates[:, :d] * y_pool + gates[:, d:] * y_att
        pending = (rows, jnp.dot(merged.astype(BF16), wo_ref[...], preferred_element_type=F32))
    finish(*pending)


def _mix_out(x1, o, yp, wgl, bg, wpu, wau, wo, g, b, alpha):
    m, d = x1.shape
    tm = TM_MIX_OUT
    row = lambda i: (i, 0)
    consts = [wgl, bg, wpu, wau, wo, g, b]
    return pl.pallas_call(
        functools.partial(_mix_out_kernel, alpha=alpha),
        grid=(m // tm,),
        in_specs=[pl.BlockSpec((tm, d), row), pl.BlockSpec((tm, o.shape[1]), row),
                  pl.BlockSpec((tm, yp.shape[1]), row)] + [_const_spec(c.shape) for c in consts],
        out_specs=pl.BlockSpec((tm, d), row),
        out_shape=jax.ShapeDtypeStruct((m, d), F32),
        compiler_params=pltpu.CompilerParams(dimension_semantics=("arbitrary",),
                                             vmem_limit_bytes=VMEM_LIMIT),
        name="mix_out",
    )(x1, o, yp, *consts)


def _block_diag(w_group):
    n, c, _ = w_group.shape
    out = jnp.zeros((n * c, n * c), w_group.dtype)
    for gi in range(n):
        out = out.at[gi * c:(gi + 1) * c, gi * c:(gi + 1) * c].set(w_group[gi])
    return out


def kernel(x, p, ffn1_w_gate, ffn1_w_up, ffn1_w_down, ln1_g, ln1_b, mix_w_in, mix_b_f, mix_b_gate, pool_w_group, pool_scale, pool_w_up, att_w_up, mix_w_out, ln2_g, ln2_b, ffn2_w_gate, ffn2_w_up, ffn2_w_down, ln3_g, ln3_b, ple_w_proj, ple_w_gate, ln4_g, ln4_b):
    bsz, seq, d = x.shape
    depth = p.shape[0]
    n_heads = mix_b_f.shape[1]
    pool_w = pool_scale.shape[1]
    att_w = n_heads * HEAD_DIM
    alpha = (2 * depth) ** 0.25
    masks_np, head_of_lane = _aug_lane_tables(n_heads)
    masks = jnp.asarray(masks_np)
    lane_sel = jnp.asarray(np.maximum(head_of_lane, 0))
    lane_on = jnp.asarray((head_of_lane >= 0).astype(np.float32))
    row2 = lambda v: v.reshape(1, -1)

    xf = x.reshape(bsz * seq, d)
    for i in range(depth):
        xf = _ffn_ln(xf, ffn1_w_gate[i].astype(BF16), ffn1_w_up[i].astype(BF16), ffn1_w_down[i].astype(BF16),
                     row2(ln1_g[i]), row2(ln1_b[i]), alpha)

        w_in = mix_w_in[i]
        qkv_end = pool_w + 3 * att_w
        w_f = w_in[:, qkv_end:qkv_end + n_heads]
        wf_rep = (w_f[:, lane_sel] * lane_on).astype(BF16)
        bf_rep = row2(mix_b_f[i][lane_sel] * lane_on)
        qa, ka, va, yp = _mix_in(xf.reshape(bsz, seq, d), w_in[:, :qkv_end].astype(BF16), wf_rep, bf_rep,
                                 _block_diag(pool_w_group[i]).astype(BF16), row2(pool_scale[i]), masks, n_heads)
        o = _attention(qa, ka, va)
        xf = _mix_out(xf, o.reshape(bsz * seq, att_w), yp.reshape(bsz * seq, pool_w),
                      w_in[:, qkv_end + n_heads:].astype(BF16), row2(mix_b_gate[i]),
                      pool_w_up[i].astype(BF16), att_w_up[i].astype(BF16), mix_w_out[i].astype(BF16),
                      row2(ln2_g[i]), row2(ln2_b[i]), alpha)

        xf = _ffn_ln(xf, ffn2_w_gate[i].astype(BF16), ffn2_w_up[i].astype(BF16), ffn2_w_down[i].astype(BF16),
                     row2(ln3_g[i]), row2(ln3_b[i]), alpha,
                     ple=(p[i].reshape(bsz * seq, -1), ple_w_proj[i].astype(BF16), ple_w_gate[i].astype(BF16),
                          row2(ln4_g[i]), row2(ln4_b[i])))
    return xf.reshape(bsz, seq, d)
```

```python
import functools

import jax
import jax.numpy as jnp
import numpy as np
from jax import lax
from jax.experimental import pallas as pl
from jax.experimental.pallas import tpu as pltpu

F32 = jnp.float32
BF16 = jnp.bfloat16

LN_EPS = 1e-5
POOL_WINDOWS = (2, 4, 8, 16)
HEAD_DIM = 64
LANES = 128
POOL_HALO = 16
VT_ONES = 16
VT_ROWS = HEAD_DIM + VT_ONES
AUG_W = 6
V7X_VMEM_BYTES = 64 * 1024 * 1024
VMEM_LIMIT = V7X_VMEM_BYTES * 7 // 8

FFN_CHUNK = 256
TM_FFN = 1024
FFN_SUB = 256
TM_MIX_OUT = 1024
MIX_OUT_SUB = 256
TM_MIX = 512
CUMSUM_BLOCK = 256
LOG2_E = float(np.log2(np.e))
HEADS_PER_STEP = 8
ATTN_STRIP = 256
KV_UNROLL = 2
QK_LOOKAHEAD = 4


def _layer_norm(y, g, b):
    mu = jnp.mean(y, axis=-1, keepdims=True)
    yc = y - mu
    var = jnp.mean(yc * yc, axis=-1, keepdims=True)
    return yc * lax.rsqrt(var + LN_EPS) * g + b


def _sigmoid(x):
    return 1.0 / (1.0 + jnp.exp(-x))


def _const_spec(shape):
    nd = len(shape)
    return pl.BlockSpec(shape, lambda *_: (0,) * nd, pipeline_mode=pl.Buffered(1))


def _ffn_kernel(*refs, alpha, n_chunks, with_ple):
    if with_ple:
        (x_ref, wg_ref, wu_ref, wd_ref, g_ref, b_ref, p_ref, wp_ref, wpg_ref, g2_ref, b2_ref,
         o_ref, h_ref) = refs
    else:
        x_ref, wg_ref, wu_ref, wd_ref, g_ref, b_ref, o_ref, h_ref = refs
    fc = FFN_CHUNK
    sub = h_ref.shape[1]

    def finish(rows, y):
        x = _layer_norm(alpha * x_ref[rows, :] + 0.5 * y, g_ref[...], b_ref[...])
        if with_ple:
            proj = jnp.dot(p_ref[rows, :].astype(BF16), wp_ref[...], preferred_element_type=F32)
            gate = _sigmoid(jnp.dot(x.astype(BF16), wpg_ref[...], preferred_element_type=F32))
            x = _layer_norm(alpha * x + proj * gate, g2_ref[...], b2_ref[...])
        o_ref[rows, :] = x

    pending = None
    for s in range(x_ref.shape[0] // sub):
        rows = slice(s * sub, (s + 1) * sub)
        xb = x_ref[rows, :].astype(BF16)
        hbuf = h_ref.at[s % 2]
        for c in range(n_chunks):
            cols = slice(c * fc, (c + 1) * fc)
            g = jnp.dot(xb, wg_ref[:, cols], preferred_element_type=F32)
            u = jnp.dot(xb, wu_ref[:, cols], preferred_element_type=F32)
            hbuf[:, cols] = (g * _sigmoid(g) * u).astype(BF16)
            if c == 0 and pending is not None:
                finish(*pending)
        pending = (rows, jnp.dot(hbuf[...], wd_ref[...], preferred_element_type=F32))
    finish(*pending)


def _ffn_ln(x, wg, wu, wd, g, b, alpha, ple=None):
    m, d = x.shape
    dff = wd.shape[0]
    tm = TM_FFN
    row = lambda i: (i, 0)
    in_specs = [pl.BlockSpec((tm, d), row), _const_spec(wg.shape), _const_spec(wu.shape), _const_spec(wd.shape),
                _const_spec(g.shape), _const_spec(b.shape)]
    args = [x, wg, wu, wd, g, b]
    if ple is not None:
        p, wp, wpg, g2, b2 = ple
        in_specs += [pl.BlockSpec((tm, p.shape[1]), row), _const_spec(wp.shape), _const_spec(wpg.shape),
                     _const_spec(g2.shape), _const_spec(b2.shape)]
        args += [p, wp, wpg, g2, b2]
    return pl.pallas_call(
        functools.partial(_ffn_kernel, alpha=alpha, n_chunks=dff // FFN_CHUNK, with_ple=ple is not None),
        grid=(m // tm,),
        in_specs=in_specs,
        out_specs=pl.BlockSpec((tm, d), row),
        out_shape=jax.ShapeDtypeStruct((m, d), F32),
        scratch_shapes=[pltpu.VMEM((2, FFN_SUB, dff), BF16)],
        compiler_params=pltpu.CompilerParams(dimension_semantics=("arbitrary",),
                                             vmem_limit_bytes=VMEM_LIMIT),
        name="ffn_ln_ple" if ple is not None else "ffn_ln",
    )(*args)


def _split3(x):
    hi = x.astype(BF16).astype(F32)
    r = x - hi
    mid = r.astype(BF16).astype(F32)
    lo = (r - mid).astype(BF16).astype(F32)
    return hi, mid, lo


def _mix_in_kernel(x_ref, w_ref, wf_ref, bf_ref, wbd_ref, pscale_ref, masks_ref,
                   qa_ref, ka_ref, va_ref, yp_ref, uext_ref, fcarry_ref, *, tm, n_heads, pool_w):
    si = pl.program_id(1)

    @pl.when(si == 0)
    def _():
        uext_ref[0:POOL_HALO, :] = jnp.zeros((POOL_HALO, pool_w), F32)
        fcarry_ref[...] = jnp.zeros_like(fcarry_ref)

    xb = x_ref[...].astype(BF16)
    mk = masks_ref[...]

    att_w = n_heads * HEAD_DIM
    sec_w = 2 * LANES
    sections = [(kind, c0) for kind in (2, 0, 1) for c0 in range(0, att_w, sec_w)]

    def section_dot(kind, c0):
        col = pool_w + kind * att_w + c0
        return jnp.dot(xb, w_ref[:, col:col + sec_w], preferred_element_type=F32)

    fl = jnp.dot(xb, wf_ref[...], preferred_element_type=F32) + bf_ref[...]
    u = jnp.dot(xb, w_ref[:, :pool_w], preferred_element_type=F32)
    pending = section_dot(*sections[0])

    logf = jnp.minimum(fl, 0.0) - jnp.log1p(jnp.exp(-jnp.abs(fl)))
    cb = CUMSUM_BLOCK
    tri = (lax.broadcasted_iota(jnp.int32, (cb, cb), 0)
           >= lax.broadcasted_iota(jnp.int32, (cb, cb), 1)).astype(BF16)
    terms = [t.astype(BF16) for t in _split3(logf)]
    running = fcarry_ref[...]
    blocks = []
    for r0 in range(0, tm, cb):
        blk = running
        for t in terms:
            blk = blk + jnp.dot(tri, t[r0:r0 + cb], preferred_element_type=F32)
        blocks.append(blk)
        running = blk[cb - 1:cb, :]
    fcum = jnp.concatenate(blocks, axis=0)
    fcarry_ref[...] = running

    hi, mid, lo = _split3(fcum * LOG2_E)
    aq = hi * mk[0:1] + mid * mk[1:2] + lo * mk[2:3] + (mk[3:4] + mk[4:5] + mk[5:6])
    ak = (mk[0:1] + mk[1:2] + mk[2:3]) - (hi * mk[3:4] + mid * mk[4:5] + lo * mk[5:6])
    lo_half = lax.broadcasted_iota(jnp.int32, (tm, LANES), 1) < LANES // 2
    data_lanes = (lo_half, jnp.logical_not(lo_half))

    uext_ref[POOL_HALO:POOL_HALO + tm, :] = u
    t_glob = si * tm + lax.broadcasted_iota(jnp.int32, (tm, 1), 0)
    pg = pool_w // len(POOL_WINDOWS)
    rs = []
    for gi, w in enumerate(POOL_WINDOWS):
        lanes = slice(gi * pg, (gi + 1) * pg)
        wsum = uext_ref[:, lanes]
        span = 1
        while span < w:
            wsum = wsum + pltpu.roll(wsum, span, 0)
            span *= 2
        inv_cnt = 1.0 / jnp.minimum(t_glob + 1, w).astype(F32)
        rs.append(wsum[POOL_HALO:] * inv_cnt - u[:, lanes])
    r = jnp.concatenate(rs, axis=1).astype(BF16)
    y = jnp.dot(r, wbd_ref[...], preferred_element_type=F32) * pscale_ref[...]
    yp_ref[...] = y.astype(BF16)
    uext_ref[0:POOL_HALO, :] = uext_ref[tm:tm + POOL_HALO, :]

    scale = HEAD_DIM ** -0.5 * LOG2_E
    for idx, (kind, c0) in enumerate(sections):
        zsec = pending
        if idx + 1 < len(sections):
            pending = section_dot(*sections[idx + 1])
        for jj in range(sec_w // LANES):
            zp = zsec[:, jj * LANES:(jj + 1) * LANES]
            for e in range(2):
                h = 2 * (c0 // LANES + jj) + e
                if kind == 0:
                    qa_ref[h] = jnp.where(data_lanes[e], zp * scale, aq).T.astype(BF16)
                elif kind == 1:
                    ka_ref[h] = jnp.where(data_lanes[e], zp, ak * mk[AUG_W + h:AUG_W + h + 1]).astype(BF16)
                else:
                    v_t = jnp.where(data_lanes[e], zp, 1.0).T
                    r0 = 0 if e == 0 else LANES - VT_ROWS
                    va_ref[h] = v_t[r0:r0 + VT_ROWS].astype(BF16)


def _aug_lane_tables(n_heads):
    half = LANES // 2
    masks = np.zeros((AUG_W + n_heads, LANES), np.float32)
    head_of_lane = np.full((LANES,), -1, np.int64)
    for h in range(n_heads):
        j, e = divmod(h, 2)
        base = half * (1 - e) + AUG_W * j
        for pos in range(AUG_W):
            masks[pos, base + pos] = 1.0
            masks[AUG_W + h, base + pos] = 1.0
            head_of_lane[base + pos] = h
    return masks, head_of_lane


def _mix_in(x1, w_uqkv, wf_rep, bf_rep, wbd, pscale, masks, n_heads):
    bsz, seq, d = x1.shape
    tm = TM_MIX
    pool_w = wbd.shape[0]
    hspec = pl.BlockSpec((None, n_heads, tm, LANES), lambda b, s: (b, 0, s, 0))
    hshape = jax.ShapeDtypeStruct((bsz, n_heads, seq, LANES), BF16)
    tspec = pl.BlockSpec((None, n_heads, None, LANES, tm), lambda b, s: (b, 0, s, 0, 0))
    tshape = jax.ShapeDtypeStruct((bsz, n_heads, seq // tm, LANES, tm), BF16)
    vspec = pl.BlockSpec((None, n_heads, None, VT_ROWS, tm), lambda b, s: (b, 0, s, 0, 0))
    vshape = jax.ShapeDtypeStruct((bsz, n_heads, seq // tm, VT_ROWS, tm), BF16)
    return pl.pallas_call(
        functools.partial(_mix_in_kernel, tm=tm, n_heads=n_heads, pool_w=pool_w),
        grid=(bsz, seq // tm),
        in_specs=[pl.BlockSpec((None, tm, d), lambda b, s: (b, s, 0)),
                  _const_spec(w_uqkv.shape), _const_spec(wf_rep.shape), _const_spec(bf_rep.shape),
                  _const_spec(wbd.shape), _const_spec(pscale.shape), _const_spec(masks.shape)],
        out_specs=[tspec, hspec, vspec,
                   pl.BlockSpec((None, tm, pool_w), lambda b, s: (b, s, 0))],
        out_shape=[tshape, hshape, vshape, jax.ShapeDtypeStruct((bsz, seq, pool_w), BF16)],
        scratch_shapes=[pltpu.VMEM((POOL_HALO + tm, pool_w), F32), pltpu.VMEM((1, LANES), F32)],
        compiler_params=pltpu.CompilerParams(dimension_semantics=("arbitrary", "arbitrary"),
                                             vmem_limit_bytes=VMEM_LIMIT),
        name="mix_in",
    )(x1, w_uqkv, wf_rep, bf_rep, wbd, pscale, masks)


def _attn_kernel(qt_ref, ka_ref, vt_ref, o_ref, m_ref, acc_ref, *, tq, hps):
    qi = pl.program_id(2)

    qw = ATTN_STRIP
    strips = [slice(c, c + qw) for c in range(0, tq, qw)]

    def scores_t(e, kb, cols, nkeys):
        k0 = pl.multiple_of(kb * tq, tq)
        return jnp.dot(ka_ref[e, pl.ds(k0, nkeys), :], qt_ref[e, :, cols], preferred_element_type=F32)

    def blocks(kbs, with_diagonal=False):
        units = [(kb, e, cols, tq, False) for kb in kbs for e in range(hps) for cols in strips]
        if with_diagonal:
            units = [(qi, e, cols, cols.stop, True) for e in range(hps) for cols in strips] + units
        ss = {}
        for step in range(len(units) + QK_LOOKAHEAD):
            if step < len(units):
                kb, e, cols, nkeys, diagonal = units[step]
                ss[step] = scores_t(e, kb, cols, nkeys)
            i = step - QK_LOOKAHEAD
            if 0 <= i < len(units):
                kb, e, cols, nkeys, diagonal = units[i]
                s = ss.pop(i)
                if diagonal:
                    key_idx = lax.broadcasted_iota(jnp.int32, (nkeys, qw), 0)
                    query_idx = lax.broadcasted_iota(jnp.int32, (nkeys, qw), 1) + cols.start
                    s = jnp.where(key_idx <= query_idx, s, -jnp.inf)
                    m_new = jnp.max(s, axis=0, keepdims=True)
                else:
                    m_old = m_ref[e, :, cols]
                    m_new = jnp.maximum(m_old, jnp.max(s, axis=0, keepdims=True))
                    rescale = jnp.exp2(m_old - m_new)
                m_ref[e, :, cols] = m_new
                p_t = jnp.exp2(s - m_new).astype(BF16)
                pv = jnp.dot(vt_ref[e, kb, :, :nkeys], p_t, preferred_element_type=F32)
                acc_ref[e, :, cols] = pv if diagonal else rescale * acc_ref[e, :, cols] + pv

    for rem in range(KV_UNROLL):
        @pl.when(qi % KV_UNROLL == rem)
        def _(rem=rem):
            blocks([qi - 1 - j for j in range(rem)], with_diagonal=True)

    def body(i, carry):
        blocks([i * KV_UNROLL + r for r in range(KV_UNROLL)])
        return carry

    lax.fori_loop(0, qi // KV_UNROLL, body, 0)

    outs = []
    for e in range(hps):
        acc = acc_ref[e]
        if e % 2 == 0:
            outs.append(acc[:HEAD_DIM] / acc[HEAD_DIM:HEAD_DIM + 1])
        else:
            outs.append(acc[VT_ONES:] / acc[0:1])
    o_ref[...] = jnp.concatenate(outs, axis=0).T.astype(BF16)


def _attention(qt, ka, vt):
    bsz, n_heads, n_blk, _, tq = qt.shape
    seq = n_blk * tq
    hps = HEADS_PER_STEP
    return pl.pallas_call(
        functools.partial(_attn_kernel, tq=tq, hps=hps),
        grid=(bsz, n_heads // hps, n_blk),
        in_specs=[pl.BlockSpec((None, hps, None, LANES, tq), lambda b, j, i: (b, j, i, 0, 0)),
                  pl.BlockSpec((None, hps, seq, LANES), lambda b, j, i: (b, j, 0, 0)),
                  pl.BlockSpec((None, hps, n_blk, VT_ROWS, tq), lambda b, j, i: (b, j, 0, 0, 0))],
        out_specs=pl.BlockSpec((None, tq, hps * HEAD_DIM), lambda b, j, i: (b, i, j)),
        out_shape=jax.ShapeDtypeStruct((bsz, seq, n_heads * HEAD_DIM), BF16),
        scratch_shapes=[pltpu.VMEM((hps, 1, tq), F32), pltpu.VMEM((hps, VT_ROWS, tq), F32)],
        compiler_params=pltpu.CompilerParams(dimension_semantics=("arbitrary",) * 3,
                                             vmem_limit_bytes=VMEM_LIMIT),
        name="fox_attention",
    )(qt, ka, vt)


def _mix_out_kernel(x_ref, o_ref, yp_ref, wgl_ref, bg_ref, wpu_ref, wau_ref, wo_ref, g_ref, b_ref,
                    out_ref, *, alpha):
    d = x_ref.shape[1]
    sub = MIX_OUT_SUB

    def finish(rows, y):
        out_ref[rows, :] = _layer_norm(alpha * x_ref[rows, :] + y, g_ref[...], b_ref[...])

    pending = None
    for s in range(x_ref.shape[0] // sub):
        rows = slice(s * sub, (s + 1) * sub)
        glog = jnp.dot(x_ref[rows, :].astype(BF16), wgl_ref[...], preferred_element_type=F32)
        if pending is not None:
            finish(*pending)
        gates = _sigmoid(glog + bg_ref[...])
        y_pool = jnp.dot(yp_ref[rows, :], wpu_ref[...], preferred_element_type=F32)
        y_att = jnp.dot(o_ref[rows, :], wau_ref[...], preferred_element_type=F32)
        merged = gates[:, :d] * y_pool + gates[:, d:] * y_att
        pending = (rows, jnp.dot(merged.astype(BF16), wo_ref[...], preferred_element_type=F32))
    finish(*pending)


def _mix_out(x1, o, yp, wgl, bg, wpu, wau, wo, g, b, alpha):
    m, d = x1.shape
    tm = TM_MIX_OUT
    row = lambda i: (i, 0)
    consts = [wgl, bg, wpu, wau, wo, g, b]
    return pl.pallas_call(
        functools.partial(_mix_out_kernel, alpha=alpha),
        grid=(m // tm,),
        in_specs=[pl.BlockSpec((tm, d), row), pl.BlockSpec((tm, o.shape[1]), row),
                  pl.BlockSpec((tm, yp.shape[1]), row)] + [_const_spec(c.shape) for c in consts],
        out_specs=pl.BlockSpec((tm, d), row),
        out_shape=jax.ShapeDtypeStruct((m, d), F32),
        compiler_params=pltpu.CompilerParams(dimension_semantics=("arbitrary",),
                                             vmem_limit_bytes=VMEM_LIMIT),
        name="mix_out",
    )(x1, o, yp, *consts)


def _block_diag(w_group):
    n, c, _ = w_group.shape
    out = jnp.zeros((n * c, n * c), w_group.dtype)
    for gi in range(n):
        out = out.at[gi * c:(gi + 1) * c, gi * c:(gi + 1) * c].set(w_group[gi])
    return out


def kernel(x, p, ffn1_w_gate, ffn1_w_up, ffn1_w_down, ln1_g, ln1_b, mix_w_in, mix_b_f, mix_b_gate, pool_w_group, pool_scale, pool_w_up, att_w_up, mix_w_out, ln2_g, ln2_b, ffn2_w_gate, ffn2_w_up, ffn2_w_down, ln3_g, ln3_b, ple_w_proj, ple_w_gate, ln4_g, ln4_b):
    bsz, seq, d = x.shape
    depth = p.shape[0]
    n_heads = mix_b_f.shape[1]
    pool_w = pool_scale.shape[1]
    att_w = n_heads * HEAD_DIM
    alpha = (2 * depth) ** 0.25
    masks_np, head_of_lane = _aug_lane_tables(n_heads)
    masks = jnp.asarray(masks_np)
    lane_sel = jnp.asarray(np.maximum(head_of_lane, 0))
    lane_on = jnp.asarray((head_of_lane >= 0).astype(np.float32))
    row2 = lambda v: v.reshape(1, -1)

    xf = x.reshape(bsz * seq, d)
    for i in range(depth):
        xf = _ffn_ln(xf, ffn1_w_gate[i].astype(BF16), ffn1_w_up[i].astype(BF16), ffn1_w_down[i].astype(BF16),
                     row2(ln1_g[i]), row2(ln1_b[i]), alpha)

        w_in = mix_w_in[i]
        qkv_end = pool_w + 3 * att_w
        w_f = w_in[:, qkv_end:qkv_end + n_heads]
        wf_rep = (w_f[:, lane_sel] * lane_on).astype(BF16)
        bf_rep = row2(mix_b_f[i][lane_sel] * lane_on)
        qa, ka, va, yp = _mix_in(xf.reshape(bsz, seq, d), w_in[:, :qkv_end].astype(BF16), wf_rep, bf_rep,
                                 _block_diag(pool_w_group[i]).astype(BF16), row2(pool_scale[i]), masks, n_heads)
        o = _attention(qa, ka, va)
        xf = _mix_out(xf, o.reshape(bsz * seq, att_w), yp.reshape(bsz * seq, pool_w),
                      w_in[:, qkv_end + n_heads:].astype(BF16), row2(mix_b_gate[i]),
                      pool_w_up[i].astype(BF16), att_w_up[i].astype(BF16), mix_w_out[i].astype(BF16),
                      row2(ln2_g[i]), row2(ln2_b[i]), alpha)

        xf = _ffn_ln(xf, ffn2_w_gate[i].astype(BF16), ffn2_w_up[i].astype(BF16), ffn2_w_down[i].astype(BF16),
                     row2(ln3_g[i]), row2(ln3_b[i]), alpha,
                     ple=(p[i].reshape(bsz * seq, -1), ple_w_proj[i].astype(BF16), ple_w_gate[i].astype(BF16),
                          row2(ln4_g[i]), row2(ln4_b[i])))
    return xf.reshape(bsz, seq, d)
```

```python
import functools

import jax
import jax.numpy as jnp
import numpy as np
from jax import lax
from jax.experimental import pallas as pl
from jax.experimental.pallas import tpu as pltpu

F32 = jnp.float32
BF16 = jnp.bfloat16

LN_EPS = 1e-5
POOL_WINDOWS = (2, 4, 8, 16)
HEAD_DIM = 64
LANES = 128
POOL_HALO = 16
VT_ONES = 16
VT_ROWS = HEAD_DIM + VT_ONES
AUG_W = 6
V7X_VMEM_BYTES = 64 * 1024 * 1024
VMEM_LIMIT = V7X_VMEM_BYTES * 7 // 8

FFN_CHUNK = 256
TM_FFN = 1024
FFN_SUB = 256
TM_MIX_OUT = 1024
MIX_OUT_SUB = 256
TM_MIX = 512
CUMSUM_BLOCK = 256
LOG2_E = float(np.log2(np.e))
HEADS_PER_STEP = 8
ATTN_STRIP = 256
KV_UNROLL = 3
QK_LOOKAHEAD = 4


def _layer_norm(y, g, b):
    mu = jnp.mean(y, axis=-1, keepdims=True)
    yc = y - mu
    var = jnp.mean(yc * yc, axis=-1, keepdims=True)
    return yc * lax.rsqrt(var + LN_EPS) * g + b


def _sigmoid(x):
    return 1.0 / (1.0 + jnp.exp(-x))


def _const_spec(shape):
    nd = len(shape)
    return pl.BlockSpec(shape, lambda *_: (0,) * nd, pipeline_mode=pl.Buffered(1))


def _ffn_kernel(*refs, alpha, n_chunks, with_ple):
    if with_ple:
        (x_ref, wg_ref, wu_ref, wd_ref, g_ref, b_ref, p_ref, wp_ref, wpg_ref, g2_ref, b2_ref,
         o_ref, h_ref) = refs
    else:
        x_ref, wg_ref, wu_ref, wd_ref, g_ref, b_ref, o_ref, h_ref = refs
    fc = FFN_CHUNK
    sub = h_ref.shape[1]

    def finish(rows, y):
        x = _layer_norm(alpha * x_ref[rows, :] + 0.5 * y, g_ref[...], b_ref[...])
        if with_ple:
            proj = jnp.dot(p_ref[rows, :].astype(BF16), wp_ref[...], preferred_element_type=F32)
            gate = _sigmoid(jnp.dot(x.astype(BF16), wpg_ref[...], preferred_element_type=F32))
            x = _layer_norm(alpha * x + proj * gate, g2_ref[...], b2_ref[...])
        o_ref[rows, :] = x

    pending = None
    for s in range(x_ref.shape[0] // sub):
        rows = slice(s * sub, (s + 1) * sub)
        xb = x_ref[rows, :].astype(BF16)
        hbuf = h_ref.at[s % 2]
        for c in range(n_chunks):
            cols = slice(c * fc, (c + 1) * fc)
            g = jnp.dot(xb, wg_ref[:, cols], preferred_element_type=F32)
            u = jnp.dot(xb, wu_ref[:, cols], preferred_element_type=F32)
            hbuf[:, cols] = (g * _sigmoid(g) * u).astype(BF16)
            if c == 0 and pending is not None:
                finish(*pending)
        pending = (rows, jnp.dot(hbuf[...], wd_ref[...], preferred_element_type=F32))
    finish(*pending)


def _ffn_ln(x, wg, wu, wd, g, b, alpha, ple=None):
    m, d = x.shape
    dff = wd.shape[0]
    tm = TM_FFN
    row = lambda i: (i, 0)
    in_specs = [pl.BlockSpec((tm, d), row), _const_spec(wg.shape), _const_spec(wu.shape), _const_spec(wd.shape),
                _const_spec(g.shape), _const_spec(b.shape)]
    args = [x, wg, wu, wd, g, b]
    if ple is not None:
        p, wp, wpg, g2, b2 = ple
        in_specs += [pl.BlockSpec((tm, p.shape[1]), row), _const_spec(wp.shape), _const_spec(wpg.shape),
                     _const_spec(g2.shape), _const_spec(b2.shape)]
        args += [p, wp, wpg, g2, b2]
    return pl.pallas_call(
        functools.partial(_ffn_kernel, alpha=alpha, n_chunks=dff // FFN_CHUNK, with_ple=ple is not None),
        grid=(m // tm,),
        in_specs=in_specs,
        out_specs=pl.BlockSpec((tm, d), row),
        out_shape=jax.ShapeDtypeStruct((m, d), F32),
        scratch_shapes=[pltpu.VMEM((2, FFN_SUB, dff), BF16)],
        compiler_params=pltpu.CompilerParams(dimension_semantics=("arbitrary",),
                                             vmem_limit_bytes=VMEM_LIMIT),
        name="ffn_ln_ple" if ple is not None else "ffn_ln",
    )(*args)


def _split3(x):
    hi = x.astype(BF16).astype(F32)
    r = x - hi
    mid = r.astype(BF16).astype(F32)
    lo = (r - mid).astype(BF16).astype(F32)
    return hi, mid, lo


def _mix_in_kernel(x_ref, w_ref, wf_ref, bf_ref, wbd_ref, pscale_ref, masks_ref,
                   qa_ref, ka_ref, va_ref, yp_ref, uext_ref, fcarry_ref, *, tm, n_heads, pool_w):
    si = pl.program_id(1)

    @pl.when(si == 0)
    def _():
        uext_ref[0:POOL_HALO, :] = jnp.zeros((POOL_HALO, pool_w), F32)
        fcarry_ref[...] = jnp.zeros_like(fcarry_ref)

    xb = x_ref[...].astype(BF16)
    mk = masks_ref[...]

    att_w = n_heads * HEAD_DIM
    sec_w = 2 * LANES
    sections = [(kind, c0) for kind in (2, 0, 1) for c0 in range(0, att_w, sec_w)]

    def section_dot(kind, c0):
        col = pool_w + kind * att_w + c0
        return jnp.dot(xb, w_ref[:, col:col + sec_w], preferred_element_type=F32)

    fl = jnp.dot(xb, wf_ref[...], preferred_element_type=F32) + bf_ref[...]
    u = jnp.dot(xb, w_ref[:, :pool_w], preferred_element_type=F32)
    pending = section_dot(*sections[0])

    logf = jnp.minimum(fl, 0.0) - jnp.log1p(jnp.exp(-jnp.abs(fl)))
    cb = CUMSUM_BLOCK
    tri = (lax.broadcasted_iota(jnp.int32, (cb, cb), 0)
           >= lax.broadcasted_iota(jnp.int32, (cb, cb), 1)).astype(BF16)
    terms = [t.astype(BF16) for t in _split3(logf)]
    running = fcarry_ref[...]
    blocks = []
    for r0 in range(0, tm, cb):
        blk = running
        for t in terms:
            blk = blk + jnp.dot(tri, t[r0:r0 + cb], preferred_element_type=F32)
        blocks.append(blk)
        running = blk[cb - 1:cb, :]
    fcum = jnp.concatenate(blocks, axis=0)
    fcarry_ref[...] = running

    hi, mid, lo = _split3(fcum * LOG2_E)
    aq = hi * mk[0:1] + mid * mk[1:2] + lo * mk[2:3] + (mk[3:4] + mk[4:5] + mk[5:6])
    ak = (mk[0:1] + mk[1:2] + mk[2:3]) - (hi * mk[3:4] + mid * mk[4:5] + lo * mk[5:6])
    lo_half = lax.broadcasted_iota(jnp.int32, (tm, LANES), 1) < LANES // 2
    data_lanes = (lo_half, jnp.logical_not(lo_half))

    uext_ref[POOL_HALO:POOL_HALO + tm, :] = u
    t_glob = si * tm + lax.broadcasted_iota(jnp.int32, (tm, 1), 0)
    pg = pool_w // len(POOL_WINDOWS)
    rs = []
    for gi, w in enumerate(POOL_WINDOWS):
        lanes = slice(gi * pg, (gi + 1) * pg)
        wsum = uext_ref[:, lanes]
        span = 1
        while span < w:
            wsum = wsum + pltpu.roll(wsum, span, 0)
            span *= 2
        inv_cnt = 1.0 / jnp.minimum(t_glob + 1, w).astype(F32)
        rs.append(wsum[POOL_HALO:] * inv_cnt - u[:, lanes])
    r = jnp.concatenate(rs, axis=1).astype(BF16)
    y = jnp.dot(r, wbd_ref[...], preferred_element_type=F32) * pscale_ref[...]
    yp_ref[...] = y.astype(BF16)
    uext_ref[0:POOL_HALO, :] = uext_ref[tm:tm + POOL_HALO, :]

    scale = HEAD_DIM ** -0.5 * LOG2_E
    for idx, (kind, c0) in enumerate(sections):
        zsec = pending
        if idx + 1 < len(sections):
            pending = section_dot(*sections[idx + 1])
        for jj in range(sec_w // LANES):
            zp = zsec[:, jj * LANES:(jj + 1) * LANES]
            for e in range(2):
                h = 2 * (c0 // LANES + jj) + e
                if kind == 0:
                    qa_ref[h] = jnp.where(data_lanes[e], zp * scale, aq).T.astype(BF16)
                elif kind == 1:
                    ka_ref[h] = jnp.where(data_lanes[e], zp, ak * mk[AUG_W + h:AUG_W + h + 1]).astype(BF16)
                else:
                    v_t = jnp.where(data_lanes[e], zp, 1.0).T
                    r0 = 0 if e == 0 else LANES - VT_ROWS
                    va_ref[h] = v_t[r0:r0 + VT_ROWS].astype(BF16)


def _aug_lane_tables(n_heads):
    half = LANES // 2
    masks = np.zeros((AUG_W + n_heads, LANES), np.float32)
    head_of_lane = np.full((LANES,), -1, np.int64)
    for h in range(n_heads):
        j, e = divmod(h, 2)
        base = half * (1 - e) + AUG_W * j
        for pos in range(AUG_W):
            masks[pos, base + pos] = 1.0
            masks[AUG_W + h, base + pos] = 1.0
            head_of_lane[base + pos] = h
    return masks, head_of_lane


def _mix_in(x1, w_uqkv, wf_rep, bf_rep, wbd, pscale, masks, n_heads):
    bsz, seq, d = x1.shape
    tm = TM_MIX
    pool_w = wbd.shape[0]
    hspec = pl.BlockSpec((None, n_heads, tm, LANES), lambda b, s: (b, 0, s, 0))
    hshape = jax.ShapeDtypeStruct((bsz, n_heads, seq, LANES), BF16)
    tspec = pl.BlockSpec((None, n_heads, None, LANES, tm), lambda b, s: (b, 0, s, 0, 0))
    tshape = jax.ShapeDtypeStruct((bsz, n_heads, seq // tm, LANES, tm), BF16)
    vspec = pl.BlockSpec((None, n_heads, None, VT_ROWS, tm), lambda b, s: (b, 0, s, 0, 0))
    vshape = jax.ShapeDtypeStruct((bsz, n_heads, seq // tm, VT_ROWS, tm), BF16)
    return pl.pallas_call(
        functools.partial(_mix_in_kernel, tm=tm, n_heads=n_heads, pool_w=pool_w),
        grid=(bsz, seq // tm),
        in_specs=[pl.BlockSpec((None, tm, d), lambda b, s: (b, s, 0)),
                  _const_spec(w_uqkv.shape), _const_spec(wf_rep.shape), _const_spec(bf_rep.shape),
                  _const_spec(wbd.shape), _const_spec(pscale.shape), _const_spec(masks.shape)],
        out_specs=[tspec, hspec, vspec,
                   pl.BlockSpec((None, tm, pool_w), lambda b, s: (b, s, 0))],
        out_shape=[tshape, hshape, vshape, jax.ShapeDtypeStruct((bsz, seq, pool_w), BF16)],
        scratch_shapes=[pltpu.VMEM((POOL_HALO + tm, pool_w), F32), pltpu.VMEM((1, LANES), F32)],
        compiler_params=pltpu.CompilerParams(dimension_semantics=("arbitrary", "arbitrary"),
                                             vmem_limit_bytes=VMEM_LIMIT),
        name="mix_in",
    )(x1, w_uqkv, wf_rep, bf_rep, wbd, pscale, masks)


def _attn_kernel(qt_ref, ka_ref, vt_ref, o_ref, m_ref, acc_ref, *, tq, hps):
    qi = pl.program_id(2)

    qw = ATTN_STRIP
    strips = [slice(c, c + qw) for c in range(0, tq, qw)]

    def scores_t(e, kb, cols, nkeys):
        k0 = pl.multiple_of(kb * tq, tq)
        return jnp.dot(ka_ref[e, pl.ds(k0, nkeys), :], qt_ref[e, :, cols], preferred_element_type=F32)

    def blocks(kbs, with_diagonal=False):
        units = [(kb, e, cols, tq, False) for kb in kbs for e in range(hps) for cols in strips]
        if with_diagonal:
            units = [(qi, e, cols, cols.stop, True) for e in range(hps) for cols in strips] + units
        ss = {}
        for step in range(len(units) + QK_LOOKAHEAD):
            if step < len(units):
                kb, e, cols, nkeys, diagonal = units[step]
                ss[step] = scores_t(e, kb, cols, nkeys)
            i = step - QK_LOOKAHEAD
            if 0 <= i < len(units):
                kb, e, cols, nkeys, diagonal = units[i]
                s = ss.pop(i)
                if diagonal:
                    key_idx = lax.broadcasted_iota(jnp.int32, (nkeys, qw), 0)
                    query_idx = lax.broadcasted_iota(jnp.int32, (nkeys, qw), 1) + cols.start
                    s = jnp.where(key_idx <= query_idx, s, -jnp.inf)
                    m_new = jnp.max(s, axis=0, keepdims=True)
                else:
                    m_old = m_ref[e, :, cols]
                    m_new = jnp.maximum(m_old, jnp.max(s, axis=0, keepdims=True))
                    rescale = jnp.exp2(m_old - m_new)
                m_ref[e, :, cols] = m_new
                p_t = jnp.exp2(s - m_new).astype(BF16)
                pv = jnp.dot(vt_ref[e, kb, :, :nkeys], p_t, preferred_element_type=F32)
                acc_ref[e, :, cols] = pv if diagonal else rescale * acc_ref[e, :, cols] + pv

    for rem in range(KV_UNROLL):
        @pl.when(qi % KV_UNROLL == rem)
        def _(rem=rem):
            blocks([qi - 1 - j for j in range(rem)], with_diagonal=True)

    def body(i, carry):
        blocks([i * KV_UNROLL + r for r in range(KV_UNROLL)])
        return carry

    lax.fori_loop(0, qi // KV_UNROLL, body, 0)

    outs = []
    for e in range(hps):
        acc = acc_ref[e]
        if e % 2 == 0:
            outs.append(acc[:HEAD_DIM] / acc[HEAD_DIM:HEAD_DIM + 1])
        else:
            outs.append(acc[VT_ONES:] / acc[0:1])
    o_ref[...] = jnp.concatenate(outs, axis=0).T.astype(BF16)


def _attention(qt, ka, vt):
    bsz, n_heads, n_blk, _, tq = qt.shape
    seq = n_blk * tq
    hps = HEADS_PER_STEP
    return pl.pallas_call(
        functools.partial(_attn_kernel, tq=tq, hps=hps),
        grid=(bsz, n_heads // hps, n_blk),
        in_specs=[pl.BlockSpec((None, hps, None, LANES, tq), lambda b, j, i: (b, j, i, 0, 0)),
                  pl.BlockSpec((None, hps, seq, LANES), lambda b, j, i: (b, j, 0, 0)),
                  pl.BlockSpec((None, hps, n_blk, VT_ROWS, tq), lambda b, j, i: (b, j, 0, 0, 0))],
        out_specs=pl.BlockSpec((None, tq, hps * HEAD_DIM), lambda b, j, i: (b, i, j)),
        out_shape=jax.ShapeDtypeStruct((bsz, seq, n_heads * HEAD_DIM), BF16),
        scratch_shapes=[pltpu.VMEM((hps, 1, tq), F32), pltpu.VMEM((hps, VT_ROWS, tq), F32)],
        compiler_params=pltpu.CompilerParams(dimension_semantics=("arbitrary",) * 3,
                                             vmem_limit_bytes=VMEM_LIMIT),
        name="fox_attention",
    )(qt, ka, vt)


def _mix_out_kernel(x_ref, o_ref, yp_ref, wgl_ref, bg_ref, wpu_ref, wau_ref, wo_ref, g_ref, b_ref,
                    out_ref, *, alpha):
    d = x_ref.shape[1]
    sub = MIX_OUT_SUB

    def finish(rows, y):
        out_ref[rows, :] = _layer_norm(alpha * x_ref[rows, :] + y, g_ref[...], b_ref[...])

    pending = None
    for s in range(x_ref.shape[0] // sub):
        rows = slice(s * sub, (s + 1) * sub)
        glog = jnp.dot(x_ref[rows, :].astype(BF16), wgl_ref[...], preferred_element_type=F32)
        if pending is not None:
            finish(*pending)
        gates = _sigmoid(glog + bg_ref[...])
        y_pool = jnp.dot(yp_ref[rows, :], wpu_ref[...], preferred_element_type=F32)
        y_att = jnp.dot(o_ref[rows, :], wau_ref[...], preferred_element_type=F32)
        merged = gates[:, :d] * y_pool + gates[:, d:] * y_att
        pending = (rows, jnp.dot(merged.astype(BF16), wo_ref[...], preferred_element_type=F32))
    finish(*pending)


def _mix_out(x1, o, yp, wgl, bg, wpu, wau, wo, g, b, alpha):
    m, d = x1.shape
    tm = TM_MIX_OUT
    row = lambda i: (i, 0)
    consts = [wgl, bg, wpu, wau, wo, g, b]
    return pl.pallas_call(
        functools.partial(_mix_out_kernel, alpha=alpha),
        grid=(m // tm,),
        in_specs=[pl.BlockSpec((tm, d), row), pl.BlockSpec((tm, o.shape[1]), row),
                  pl.BlockSpec((tm, yp.shape[1]), row)] + [_const_spec(c.shape) for c in consts],
        out_specs=pl.BlockSpec((tm, d), row),
        out_shape=jax.ShapeDtypeStruct((m, d), F32),
        compiler_params=pltpu.CompilerParams(dimension_semantics=("arbitrary",),
                                             vmem_limit_bytes=VMEM_LIMIT),
        name="mix_out",
    )(x1, o, yp, *consts)


def _block_diag(w_group):
    n, c, _ = w_group.shape
    out = jnp.zeros((n * c, n * c), w_group.dtype)
    for gi in range(n):
        out = out.at[gi * c:(gi + 1) * c, gi * c:(gi + 1) * c].set(w_group[gi])
    return out


def kernel(x, p, ffn1_w_gate, ffn1_w_up, ffn1_w_down, ln1_g, ln1_b, mix_w_in, mix_b_f, mix_b_gate, pool_w_group, pool_scale, pool_w_up, att_w_up, mix_w_out, ln2_g, ln2_b, ffn2_w_gate, ffn2_w_up, ffn2_w_down, ln3_g, ln3_b, ple_w_proj, ple_w_gate, ln4_g, ln4_b):
    bsz, seq, d = x.shape
    depth = p.shape[0]
    n_heads = mix_b_f.shape[1]
    pool_w = pool_scale.shape[1]
    att_w = n_heads * HEAD_DIM
    alpha = (2 * depth) ** 0.25
    masks_np, head_of_lane = _aug_lane_tables(n_heads)
    masks = jnp.asarray(masks_np)
    lane_sel = jnp.asarray(np.maximum(head_of_lane, 0))
    lane_on = jnp.asarray((head_of_lane >= 0).astype(np.float32))
    row2 = lambda v: v.reshape(1, -1)

    xf = x.reshape(bsz * seq, d)
    for i in range(depth):
        xf = _ffn_ln(xf, ffn1_w_gate[i].astype(BF16), ffn1_w_up[i].astype(BF16), ffn1_w_down[i].astype(BF16),
                     row2(ln1_g[i]), row2(ln1_b[i]), alpha)

        w_in = mix_w_in[i]
        qkv_end = pool_w + 3 * att_w
        w_f = w_in[:, qkv_end:qkv_end + n_heads]
        wf_rep = (w_f[:, lane_sel] * lane_on).astype(BF16)
        bf_rep = row2(mix_b_f[i][lane_sel] * lane_on)
        qa, ka, va, yp = _mix_in(xf.reshape(bsz, seq, d), w_in[:, :qkv_end].astype(BF16), wf_rep, bf_rep,
                                 _block_diag(pool_w_group[i]).astype(BF16), row2(pool_scale[i]), masks, n_heads)
        o = _attention(qa, ka, va)
        xf = _mix_out(xf, o.reshape(bsz * seq, att_w), yp.reshape(bsz * seq, pool_w),
                      w_in[:, qkv_end + n_heads:].astype(BF16), row2(mix_b_gate[i]),
                      pool_w_up[i].astype(BF16), att_w_up[i].astype(BF16), mix_w_out[i].astype(BF16),
                      row2(ln2_g[i]), row2(ln2_b[i]), alpha)

        xf = _ffn_ln(xf, ffn2_w_gate[i].astype(BF16), ffn2_w_up[i].astype(BF16), ffn2_w_down[i].astype(BF16),
                     row2(ln3_g[i]), row2(ln3_b[i]), alpha,
                     ple=(p[i].reshape(bsz * seq, -1), ple_w_proj[i].astype(BF16), ple_w_gate[i].astype(BF16),
                          row2(ln4_g[i]), row2(ln4_b[i])))
    return xf.reshape(bsz, seq, d)
```

```python
import functools

import jax
import jax.numpy as jnp
import numpy as np
from jax import lax
from jax.experimental import pallas as pl
from jax.experimental.pallas import tpu as pltpu

F32 = jnp.float32
BF16 = jnp.bfloat16

LN_EPS = 1e-5
POOL_WINDOWS = (2, 4, 8, 16)
HEAD_DIM = 64
LANES = 128
POOL_HALO = 16
VT_ONES = 16
VT_ROWS = HEAD_DIM + VT_ONES
AUG_W = 6
V7X_VMEM_BYTES = 64 * 1024 * 1024
VMEM_LIMIT = V7X_VMEM_BYTES * 7 // 8

FFN_CHUNK = 256
TM_FFN = 1024
FFN_SUB = 256
TM_MIX_OUT = 1024
MIX_OUT_SUB = 256
TM_MIX = 512
SECTION_LOOKAHEAD = 2
CUMSUM_BLOCK = 256
LOG2_E = float(np.log2(np.e))
HEADS_PER_STEP = 8
ATTN_STRIP = 256
KV_UNROLL = 2
QK_LOOKAHEAD = 4


def _layer_norm(y, g, b):
    mu = jnp.mean(y, axis=-1, keepdims=True)
    yc = y - mu
    var = jnp.mean(yc * yc, axis=-1, keepdims=True)
    return yc * lax.rsqrt(var + LN_EPS) * g + b


def _sigmoid(x):
    return 1.0 / (1.0 + jnp.exp(-x))


def _const_spec(shape):
    nd = len(shape)
    return pl.BlockSpec(shape, lambda *_: (0,) * nd, pipeline_mode=pl.Buffered(1))


def _ffn_kernel(*refs, alpha, n_chunks, with_ple):
    if with_ple:
        (x_ref, wg_ref, wu_ref, wd_ref, g_ref, b_ref, p_ref, wp_ref, wpg_ref, g2_ref, b2_ref,
         o_ref, h_ref) = refs
    else:
        x_ref, wg_ref, wu_ref, wd_ref, g_ref, b_ref, o_ref, h_ref = refs
    fc = FFN_CHUNK
    sub = h_ref.shape[1]

    def finish(rows, y):
        x = _layer_norm(alpha * x_ref[rows, :] + 0.5 * y, g_ref[...], b_ref[...])
        if with_ple:
            proj = jnp.dot(p_ref[rows, :].astype(BF16), wp_ref[...], preferred_element_type=F32)
            gate = _sigmoid(jnp.dot(x.astype(BF16), wpg_ref[...], preferred_element_type=F32))
            x = _layer_norm(alpha * x + proj * gate, g2_ref[...], b2_ref[...])
        o_ref[rows, :] = x

    pending = None
    for s in range(x_ref.shape[0] // sub):
        rows = slice(s * sub, (s + 1) * sub)
        xb = x_ref[rows, :].astype(BF16)
        hbuf = h_ref.at[s % 2]
        for c in range(n_chunks):
            cols = slice(c * fc, (c + 1) * fc)
            g = jnp.dot(xb, wg_ref[:, cols], preferred_element_type=F32)
            u = jnp.dot(xb, wu_ref[:, cols], preferred_element_type=F32)
            hbuf[:, cols] = (g * _sigmoid(g) * u).astype(BF16)
            if c == 0 and pending is not None:
                finish(*pending)
        pending = (rows, jnp.dot(hbuf[...], wd_ref[...], preferred_element_type=F32))
    finish(*pending)


def _ffn_ln(x, wg, wu, wd, g, b, alpha, ple=None):
    m, d = x.shape
    dff = wd.shape[0]
    tm = TM_FFN
    row = lambda i: (i, 0)
    in_specs = [pl.BlockSpec((tm, d), row), _const_spec(wg.shape), _const_spec(wu.shape), _const_spec(wd.shape),
                _const_spec(g.shape), _const_spec(b.shape)]
    args = [x, wg, wu, wd, g, b]
    if ple is not None:
        p, wp, wpg, g2, b2 = ple
        in_specs += [pl.BlockSpec((tm, p.shape[1]), row), _const_spec(wp.shape), _const_spec(wpg.shape),
                     _const_spec(g2.shape), _const_spec(b2.shape)]
        args += [p, wp, wpg, g2, b2]
    return pl.pallas_call(
        functools.partial(_ffn_kernel, alpha=alpha, n_chunks=dff // FFN_CHUNK, with_ple=ple is not None),
        grid=(m // tm,),
        in_specs=in_specs,
        out_specs=pl.BlockSpec((tm, d), row),
        out_shape=jax.ShapeDtypeStruct((m, d), F32),
        scratch_shapes=[pltpu.VMEM((2, FFN_SUB, dff), BF16)],
        compiler_params=pltpu.CompilerParams(dimension_semantics=("arbitrary",),
                                             vmem_limit_bytes=VMEM_LIMIT),
        name="ffn_ln_ple" if ple is not None else "ffn_ln",
    )(*args)


def _split3(x):
    hi = x.astype(BF16).astype(F32)
    r = x - hi
    mid = r.astype(BF16).astype(F32)
    lo = (r - mid).astype(BF16).astype(F32)
    return hi, mid, lo


def _mix_in_kernel(x_ref, w_ref, wf_ref, bf_ref, wbd_ref, pscale_ref, masks_ref,
                   qa_ref, ka_ref, va_ref, yp_ref, uext_ref, fcarry_ref, *, tm, n_heads, pool_w):
    si = pl.program_id(1)

    @pl.when(si == 0)
    def _():
        uext_ref[0:POOL_HALO, :] = jnp.zeros((POOL_HALO, pool_w), F32)
        fcarry_ref[...] = jnp.zeros_like(fcarry_ref)

    xb = x_ref[...].astype(BF16)
    mk = masks_ref[...]

    att_w = n_heads * HEAD_DIM
    sec_w = 2 * LANES
    sections = [(kind, c0) for kind in (2, 0, 1) for c0 in range(0, att_w, sec_w)]

    def section_dot(kind, c0):
        col = pool_w + kind * att_w + c0
        return jnp.dot(xb, w_ref[:, col:col + sec_w], preferred_element_type=F32)

    fl = jnp.dot(xb, wf_ref[...], preferred_element_type=F32) + bf_ref[...]
    u = jnp.dot(xb, w_ref[:, :pool_w], preferred_element_type=F32)
    pending = [section_dot(*sec) for sec in sections[:SECTION_LOOKAHEAD]]

    logf = jnp.minimum(fl, 0.0) - jnp.log1p(jnp.exp(-jnp.abs(fl)))
    cb = CUMSUM_BLOCK
    tri = (lax.broadcasted_iota(jnp.int32, (cb, cb), 0)
           >= lax.broadcasted_iota(jnp.int32, (cb, cb), 1)).astype(BF16)
    terms = [t.astype(BF16) for t in _split3(logf)]
    running = fcarry_ref[...]
    blocks = []
    for r0 in range(0, tm, cb):
        blk = running
        for t in terms:
            blk = blk + jnp.dot(tri, t[r0:r0 + cb], preferred_element_type=F32)
        blocks.append(blk)
        running = blk[cb - 1:cb, :]
    fcum = jnp.concatenate(blocks, axis=0)
    fcarry_ref[...] = running

    hi, mid, lo = _split3(fcum * LOG2_E)
    aq = hi * mk[0:1] + mid * mk[1:2] + lo * mk[2:3] + (mk[3:4] + mk[4:5] + mk[5:6])
    ak = (mk[0:1] + mk[1:2] + mk[2:3]) - (hi * mk[3:4] + mid * mk[4:5] + lo * mk[5:6])
    lo_half = lax.broadcasted_iota(jnp.int32, (tm, LANES), 1) < LANES // 2
    data_lanes = (lo_half, jnp.logical_not(lo_half))

    uext_ref[POOL_HALO:POOL_HALO + tm, :] = u
    t_glob = si * tm + lax.broadcasted_iota(jnp.int32, (tm, 1), 0)
    pg = pool_w // len(POOL_WINDOWS)
    rs = []
    for gi, w in enumerate(POOL_WINDOWS):
        lanes = slice(gi * pg, (gi + 1) * pg)
        wsum = uext_ref[:, lanes]
        span = 1
        while span < w:
            wsum = wsum + pltpu.roll(wsum, span, 0)
            span *= 2
        inv_cnt = 1.0 / jnp.minimum(t_glob + 1, w).astype(F32)
        rs.append(wsum[POOL_HALO:] * inv_cnt - u[:, lanes])
    r = jnp.concatenate(rs, axis=1).astype(BF16)
    y = jnp.dot(r, wbd_ref[...], preferred_element_type=F32) * pscale_ref[...]
    yp_ref[...] = y.astype(BF16)
    uext_ref[0:POOL_HALO, :] = uext_ref[tm:tm + POOL_HALO, :]

    scale = HEAD_DIM ** -0.5 * LOG2_E
    for idx, (kind, c0) in enumerate(sections):
        zsec = pending.pop(0)
        if idx + SECTION_LOOKAHEAD < len(sections):
            pending.append(section_dot(*sections[idx + SECTION_LOOKAHEAD]))
        for jj in range(sec_w // LANES):
            zp = zsec[:, jj * LANES:(jj + 1) * LANES]
            for e in range(2):
                h = 2 * (c0 // LANES + jj) + e
                if kind == 0:
                    qa_ref[h] = jnp.where(data_lanes[e], zp * scale, aq).T.astype(BF16)
                elif kind == 1:
                    ka_ref[h] = jnp.where(data_lanes[e], zp, ak * mk[AUG_W + h:AUG_W + h + 1]).astype(BF16)
                else:
                    v_t = jnp.where(data_lanes[e], zp, 1.0).T
                    r0 = 0 if e == 0 else LANES - VT_ROWS
                    va_ref[h] = v_t[r0:r0 + VT_ROWS].astype(BF16)


def _aug_lane_tables(n_heads):
    half = LANES // 2
    masks = np.zeros((AUG_W + n_heads, LANES), np.float32)
    head_of_lane = np.full((LANES,), -1, np.int64)
    for h in range(n_heads):
        j, e = divmod(h, 2)
        base = half * (1 - e) + AUG_W * j
        for pos in range(AUG_W):
            masks[pos, base + pos] = 1.0
            masks[AUG_W + h, base + pos] = 1.0
            head_of_lane[base + pos] = h
    return masks, head_of_lane


def _mix_in(x1, w_uqkv, wf_rep, bf_rep, wbd, pscale, masks, n_heads):
    bsz, seq, d = x1.shape
    tm = TM_MIX
    pool_w = wbd.shape[0]
    hspec = pl.BlockSpec((None, n_heads, tm, LANES), lambda b, s: (b, 0, s, 0))
    hshape = jax.ShapeDtypeStruct((bsz, n_heads, seq, LANES), BF16)
    tspec = pl.BlockSpec((None, n_heads, None, LANES, tm), lambda b, s: (b, 0, s, 0, 0))
    tshape = jax.ShapeDtypeStruct((bsz, n_heads, seq // tm, LANES, tm), BF16)
    vspec = pl.BlockSpec((None, n_heads, None, VT_ROWS, tm), lambda b, s: (b, 0, s, 0, 0))
    vshape = jax.ShapeDtypeStruct((bsz, n_heads, seq // tm, VT_ROWS, tm), BF16)
    return pl.pallas_call(
        functools.partial(_mix_in_kernel, tm=tm, n_heads=n_heads, pool_w=pool_w),
        grid=(bsz, seq // tm),
        in_specs=[pl.BlockSpec((None, tm, d), lambda b, s: (b, s, 0)),
                  _const_spec(w_uqkv.shape), _const_spec(wf_rep.shape), _const_spec(bf_rep.shape),
                  _const_spec(wbd.shape), _const_spec(pscale.shape), _const_spec(masks.shape)],
        out_specs=[tspec, hspec, vspec,
                   pl.BlockSpec((None, tm, pool_w), lambda b, s: (b, s, 0))],
        out_shape=[tshape, hshape, vshape, jax.ShapeDtypeStruct((bsz, seq, pool_w), BF16)],
        scratch_shapes=[pltpu.VMEM((POOL_HALO + tm, pool_w), F32), pltpu.VMEM((1, LANES), F32)],
        compiler_params=pltpu.CompilerParams(dimension_semantics=("arbitrary", "arbitrary"),
                                             vmem_limit_bytes=VMEM_LIMIT),
        name="mix_in",
    )(x1, w_uqkv, wf_rep, bf_rep, wbd, pscale, masks)


def _attn_kernel(qt_ref, ka_ref, vt_ref, o_ref, m_ref, acc_ref, *, tq, hps):
    qi = pl.program_id(2)

    qw = ATTN_STRIP
    strips = [slice(c, c + qw) for c in range(0, tq, qw)]

    def scores_t(e, kb, cols, nkeys):
        k0 = pl.multiple_of(kb * tq, tq)
        return jnp.dot(ka_ref[e, pl.ds(k0, nkeys), :], qt_ref[e, :, cols], preferred_element_type=F32)

    def blocks(kbs, with_diagonal=False):
        units = [(kb, e, cols, tq, False) for kb in kbs for e in range(hps) for cols in strips]
        if with_diagonal:
            units = [(qi, e, cols, cols.stop, True) for e in range(hps) for cols in strips] + units
        ss = {}
        for step in range(len(units) + QK_LOOKAHEAD):
            if step < len(units):
                kb, e, cols, nkeys, diagonal = units[step]
                ss[step] = scores_t(e, kb, cols, nkeys)
            i = step - QK_LOOKAHEAD
            if 0 <= i < len(units):
                kb, e, cols, nkeys, diagonal = units[i]
                s = ss.pop(i)
                if diagonal:
                    key_idx = lax.broadcasted_iota(jnp.int32, (nkeys, qw), 0)
                    query_idx = lax.broadcasted_iota(jnp.int32, (nkeys, qw), 1) + cols.start
                    s = jnp.where(key_idx <= query_idx, s, -jnp.inf)
                    m_new = jnp.max(s, axis=0, keepdims=True)
                else:
                    m_old = m_ref[e, :, cols]
                    m_new = jnp.maximum(m_old, jnp.max(s, axis=0, keepdims=True))
                    rescale = jnp.exp2(m_old - m_new)
                m_ref[e, :, cols] = m_new
                p_t = jnp.exp2(s - m_new).astype(BF16)
                pv = jnp.dot(vt_ref[e, kb, :, :nkeys], p_t, preferred_element_type=F32)
                acc_ref[e, :, cols] = pv if diagonal else rescale * acc_ref[e, :, cols] + pv

    for rem in range(KV_UNROLL):
        @pl.when(qi % KV_UNROLL == rem)
        def _(rem=rem):
            blocks([qi - 1 - j for j in range(rem)], with_diagonal=True)

    def body(i, carry):
        blocks([i * KV_UNROLL + r for r in range(KV_UNROLL)])
        return carry

    lax.fori_loop(0, qi // KV_UNROLL, body, 0)

    outs = []
    for e in range(hps):
        acc = acc_ref[e]
        if e % 2 == 0:
            outs.append(acc[:HEAD_DIM] / acc[HEAD_DIM:HEAD_DIM + 1])
        else:
            outs.append(acc[VT_ONES:] / acc[0:1])
    o_ref[...] = jnp.concatenate(outs, axis=0).T.astype(BF16)


def _attention(qt, ka, vt):
    bsz, n_heads, n_blk, _, tq = qt.shape
    seq = n_blk * tq
    hps = HEADS_PER_STEP
    return pl.pallas_call(
        functools.partial(_attn_kernel, tq=tq, hps=hps),
        grid=(bsz, n_heads // hps, n_blk),
        in_specs=[pl.BlockSpec((None, hps, None, LANES, tq), lambda b, j, i: (b, j, i, 0, 0)),
                  pl.BlockSpec((None, hps, seq, LANES), lambda b, j, i: (b, j, 0, 0)),
                  pl.BlockSpec((None, hps, n_blk, VT_ROWS, tq), lambda b, j, i: (b, j, 0, 0, 0))],
        out_specs=pl.BlockSpec((None, tq, hps * HEAD_DIM), lambda b, j, i: (b, i, j)),
        out_shape=jax.ShapeDtypeStruct((bsz, seq, n_heads * HEAD_DIM), BF16),
        scratch_shapes=[pltpu.VMEM((hps, 1, tq), F32), pltpu.VMEM((hps, VT_ROWS, tq), F32)],
        compiler_params=pltpu.CompilerParams(dimension_semantics=("arbitrary",) * 3,
                                             vmem_limit_bytes=VMEM_LIMIT),
        name="fox_attention",
    )(qt, ka, vt)


def _mix_out_kernel(x_ref, o_ref, yp_ref, wgl_ref, bg_ref, wpu_ref, wau_ref, wo_ref, g_ref, b_ref,
                    out_ref, *, alpha):
    d = x_ref.shape[1]
    sub = MIX_OUT_SUB

    def finish(rows, y):
        out_ref[rows, :] = _layer_norm(alpha * x_ref[rows, :] + y, g_ref[...], b_ref[...])

    pending = None
    for s in range(x_ref.shape[0] // sub):
        rows = slice(s * sub, (s + 1) * sub)
        glog = jnp.dot(x_ref[rows, :].astype(BF16), wgl_ref[...], preferred_element_type=F32)
        if pending is not None:
            finish(*pending)
        gates = _sigmoid(glog + bg_ref[...])
        y_pool = jnp.dot(yp_ref[rows, :], wpu_ref[...], preferred_element_type=F32)
        y_att = jnp.dot(o_ref[rows, :], wau_ref[...], preferred_element_type=F32)
        merged = gates[:, :d] * y_pool + gates[:, d:] * y_att
        pending = (rows, jnp.dot(merged.astype(BF16), wo_ref[...], preferred_element_type=F32))
    finish(*pending)


def _mix_out(x1, o, yp, wgl, bg, wpu, wau, wo, g, b, alpha):
    m, d = x1.shape
    tm = TM_MIX_OUT
    row = lambda i: (i, 0)
    consts = [wgl, bg, wpu, wau, wo, g, b]
    return pl.pallas_call(
        functools.partial(_mix_out_kernel, alpha=alpha),
        grid=(m // tm,),
        in_specs=[pl.BlockSpec((tm, d), row), pl.BlockSpec((tm, o.shape[1]), row),
                  pl.BlockSpec((tm, yp.shape[1]), row)] + [_const_spec(c.shape) for c in consts],
        out_specs=pl.BlockSpec((tm, d), row),
        out_shape=jax.ShapeDtypeStruct((m, d), F32),
        compiler_params=pltpu.CompilerParams(dimension_semantics=("arbitrary",),
                                             vmem_limit_bytes=VMEM_LIMIT),
        name="mix_out",
    )(x1, o, yp, *consts)


def _block_diag(w_group):
    n, c, _ = w_group.shape
    out = jnp.zeros((n * c, n * c), w_group.dtype)
    for gi in range(n):
        out = out.at[gi * c:(gi + 1) * c, gi * c:(gi + 1) * c].set(w_group[gi])
    return out


def kernel(x, p, ffn1_w_gate, ffn1_w_up, ffn1_w_down, ln1_g, ln1_b, mix_w_in, mix_b_f, mix_b_gate, pool_w_group, pool_scale, pool_w_up, att_w_up, mix_w_out, ln2_g, ln2_b, ffn2_w_gate, ffn2_w_up, ffn2_w_down, ln3_g, ln3_b, ple_w_proj, ple_w_gate, ln4_g, ln4_b):
    bsz, seq, d = x.shape
    depth = p.shape[0]
    n_heads = mix_b_f.shape[1]
    pool_w = pool_scale.shape[1]
    att_w = n_heads * HEAD_DIM
    alpha = (2 * depth) ** 0.25
    masks_np, head_of_lane = _aug_lane_tables(n_heads)
    masks = jnp.asarray(masks_np)
    lane_sel = jnp.asarray(np.maximum(head_of_lane, 0))
    lane_on = jnp.asarray((head_of_lane >= 0).astype(np.float32))
    row2 = lambda v: v.reshape(1, -1)

    xf = x.reshape(bsz * seq, d)
    for i in range(depth):
        xf = _ffn_ln(xf, ffn1_w_gate[i].astype(BF16), ffn1_w_up[i].astype(BF16), ffn1_w_down[i].astype(BF16),
                     row2(ln1_g[i]), row2(ln1_b[i]), alpha)

        w_in = mix_w_in[i]
        qkv_end = pool_w + 3 * att_w
        w_f = w_in[:, qkv_end:qkv_end + n_heads]
        wf_rep = (w_f[:, lane_sel] * lane_on).astype(BF16)
        bf_rep = row2(mix_b_f[i][lane_sel] * lane_on)
        qa, ka, va, yp = _mix_in(xf.reshape(bsz, seq, d), w_in[:, :qkv_end].astype(BF16), wf_rep, bf_rep,
                                 _block_diag(pool_w_group[i]).astype(BF16), row2(pool_scale[i]), masks, n_heads)
        o = _attention(qa, ka, va)
        xf = _mix_out(xf, o.reshape(bsz * seq, att_w), yp.reshape(bsz * seq, pool_w),
                      w_in[:, qkv_end + n_heads:].astype(BF16), row2(mix_b_gate[i]),
                      pool_w_up[i].astype(BF16), att_w_up[i].astype(BF16), mix_w_out[i].astype(BF16),
                      row2(ln2_g[i]), row2(ln2_b[i]), alpha)

        xf = _ffn_ln(xf, ffn2_w_gate[i].astype(BF16), ffn2_w_up[i].astype(BF16), ffn2_w_down[i].astype(BF16),
                     row2(ln3_g[i]), row2(ln3_b[i]), alpha,
                     ple=(p[i].reshape(bsz * seq, -1), ple_w_proj[i].astype(BF16), ple_w_gate[i].astype(BF16),
                          row2(ln4_g[i]), row2(ln4_b[i])))
    return xf.reshape(bsz, seq, d)
```
